```python
import jax, jax.numpy as jnp
from jax import lax
import numpy as np

D_MODEL = 1024
BATCH = 2
SEQ = 8192
DEPTH = 4

HEAD_DIM = 64
RET_HEADS = (3 * D_MODEL // 8) // HEAD_DIM
MOBA_HEADS = (3 * D_MODEL // 8) // HEAD_DIM
CONV_GROUPS = (D_MODEL // 4) // HEAD_DIM
RET_WIDTH = RET_HEADS * HEAD_DIM
MOBA_WIDTH = MOBA_HEADS * HEAD_DIM
CONV_WIDTH = CONV_GROUPS * HEAD_DIM
MIX_WIDTH = RET_WIDTH + MOBA_WIDTH + CONV_WIDTH
IN_WIDTH = 4 * RET_WIDTH + 3 * MOBA_WIDTH + 2 * CONV_WIDTH
D_FF = ((8 * D_MODEL // 3 + 255) // 256) * 256
RET_CHUNK = 256
RET_ROT_BASE = 10000.0
MOBA_BLOCK = 256
MOBA_TOPK = 3
MOBA_QBLOCK = 128
ROPE_THETA = 500000.0
ROPE_DIMS = HEAD_DIM // 4
CONV_KERNEL = 31
FFN_RES_WEIGHT = 0.5
EPS = 1e-6
PAD_MULT = 256

kernel_name = "hybrid_retention_moba_conformer_macaron"


def rms_norm(x, g):
    xf = x.astype(jnp.float32)
    y = xf * lax.rsqrt(jnp.mean(xf * xf, -1, keepdims=True) + EPS)
    return (y * g.astype(jnp.float32)).astype(x.dtype)


def swiglu(h, wg, wu, wd):
    return (jax.nn.silu(h @ wg) * (h @ wu)) @ wd


def to_heads(t, n_heads):
    b, s, _ = t.shape
    return t.reshape(b, s, n_heads, HEAD_DIM).transpose(0, 2, 1, 3)


def from_heads(t):
    b, h, s, d = t.shape
    return t.transpose(0, 2, 1, 3).reshape(b, s, h * d)


def rotary(x, pos, inv_freq):
    half = inv_freq.shape[0]
    n_rot = 2 * half
    ang = pos.astype(jnp.float32)[:, None, :, None] * inv_freq
    cos, sin = jnp.cos(ang), jnp.sin(ang)
    xr = x[..., :n_rot].astype(jnp.float32)
    x1, x2 = xr[..., :half], xr[..., half:]
    rot = jnp.concatenate([x1 * cos - x2 * sin, x2 * cos + x1 * sin], -1).astype(x.dtype)
    return jnp.concatenate([rot, x[..., n_rot:]], -1)


def retention(q, k, v):
    b, h, s, d = q.shape
    c = RET_CHUNK
    n = s // c
    log_gamma = jnp.log1p(-jnp.exp2(-5.0 - jnp.arange(h, dtype=jnp.float32)))
    lg = log_gamma[:, None]
    qc = q.reshape(b, h, n, c, d)
    kc = (k * (d ** -0.5)).reshape(b, h, n, c, d)
    vc = v.reshape(b, h, n, c, d)
    idx = jnp.arange(c, dtype=jnp.float32)
    rel = idx[:, None] - idx[None, :]
    decay_in = jnp.where(rel >= 0, jnp.exp(lg[:, :, None] * jnp.maximum(rel, 0.0)), 0.0)
    scores = jnp.einsum('bhncd,bhnmd->bhncm', qc, kc) * decay_in[None, :, None]
    inner = jnp.einsum('bhncm,bhnme->bhnce', scores, vc)
    w_state = jnp.exp(lg * (c - 1 - idx))
    kv = jnp.einsum('bhncd,hc,bhnce->bhnde', kc, w_state, vc)
    chunk_decay = jnp.exp(log_gamma * c).astype(kv.dtype)[None, :, None, None]

    def step(state, kv_i):
        return state * chunk_decay + kv_i, state

    _, prev = lax.scan(step, jnp.zeros((b, h, d, d), kv.dtype), jnp.moveaxis(kv, 2, 0))
    prev = jnp.moveaxis(prev, 0, 2)
    w_cross = jnp.exp(lg * (idx + 1.0))
    cross = jnp.einsum('bhncd,bhnde->bhnce', qc, prev) * w_cross[None, :, None, :, None]
    return (inner + cross).reshape(b, h, s, d)


def moba_attention(q, k, v):
    b, h, s, d = q.shape
    bs = MOBA_BLOCK
    nb = s // bs
    qb = MOBA_QBLOCK
    nq = s // qb
    topk = min(MOBA_TOPK, nb)
    scale = d ** -0.5
    kb = k.reshape(b, h, nb, bs, d)
    vb = v.reshape(b, h, nb, bs, d)
    kmean = jnp.mean(kb.astype(jnp.float32), axis=3)
    b_i = jnp.arange(b)[:, None, None]
    h_i = jnp.arange(h)[None, :, None]
    klocal = jnp.arange(bs)

    def one_qblock(args):
        qblk, qi = args
        qpos = qi * qb + jnp.arange(qb)
        own = (qi * qb) // bs
        gate = jnp.einsum('bhqd,bhnd->bhqn', qblk.astype(jnp.float32), kmean)
        gate = jnp.where(jnp.arange(nb) < own, gate, -jnp.inf)
        _, sel = lax.top_k(gate, topk)
        valid = sel < own
        k_own = lax.dynamic_index_in_dim(kb, own, axis=2, keepdims=False)
        v_own = lax.dynamic_index_in_dim(vb, own, axis=2, keepdims=False)
        s_own = jnp.einsum('bhqd,bhkd->bhqk', qblk, k_own).astype(jnp.float32) * scale
        causal = (own * bs + klocal)[None, :] <= qpos[:, None]
        s_list = [jnp.where(causal, s_own, -jnp.inf)]
        for j in range(topk):
            k_j = kb[b_i, h_i, sel[..., j]]
            s_j = jnp.einsum('bhqd,bhqkd->bhqk', qblk, k_j).astype(jnp.float32) * scale
            s_list.append(jnp.where(valid[..., j, None], s_j, -jnp.inf))
        p = jax.nn.softmax(jnp.concatenate(s_list, -1), axis=-1).astype(v.dtype)
        out = jnp.einsum('bhqk,bhkd->bhqd', p[..., :bs], v_own)
        for j in range(topk):
            v_j = vb[b_i, h_i, sel[..., j]]
            out = out + jnp.einsum('bhqk,bhqkd->bhqd', p[..., (j + 1) * bs:(j + 2) * bs], v_j)
        return out

    qs = jnp.moveaxis(q.reshape(b, h, nq, qb, d), 2, 0)
    outs = lax.map(one_qblock, (qs, jnp.arange(nq)))
    return jnp.moveaxis(outs, 0, 2).reshape(b, h, s, d)


def conv_module(a, gate, w, bias, ln_g, ln_b):
    bsz, s, c = a.shape
    u = a * jax.nn.sigmoid(gate)
    y = lax.conv_general_dilated(u, w[:, None, :], window_strides=(1,), padding=[(CONV_KERNEL - 1, 0)],
                                 dimension_numbers=('NWC', 'WIO', 'NWC'), feature_group_count=c) + bias
    yg = y.reshape(bsz, s, CONV_GROUPS, c // CONV_GROUPS).astype(jnp.float32)
    mu = jnp.mean(yg, -1, keepdims=True)
    var = jnp.mean(jnp.square(yg - mu), -1, keepdims=True)
    yn = ((yg - mu) * lax.rsqrt(var + EPS)).reshape(bsz, s, c) * ln_g.astype(jnp.float32) + ln_b.astype(jnp.float32)
    return jax.nn.silu(yn).astype(a.dtype)


def setup_inputs(seed: int = 0) -> dict:
    key = jax.random.key(seed)
    ks = jax.random.split(key, 20)
    f32 = jnp.float32

    def w(k, shape, fan_in):
        return jax.random.normal(k, shape, f32) * (fan_in ** -0.5)

    def gain(k, shape):
        return 1.0 + 0.02 * jax.random.normal(k, shape, f32)

    return {
        "x": jax.random.normal(ks[0], (BATCH, SEQ, D_MODEL), f32),
        "positions": jnp.broadcast_to(jnp.arange(SEQ, dtype=jnp.int32), (BATCH, SEQ)),
        "ffn1_norm": gain(ks[1], (DEPTH, D_MODEL)),
        "ffn1_wg": w(ks[2], (DEPTH, D_MODEL, D_FF), D_MODEL),
        "ffn1_wu": w(ks[3], (DEPTH, D_MODEL, D_FF), D_MODEL),
        "ffn1_wd": w(ks[4], (DEPTH, D_FF, D_MODEL), D_FF),
        "mix_norm": gain(ks[5], (DEPTH, D_MODEL)),
        "w_in": w(ks[6], (DEPTH, D_MODEL, IN_WIDTH), D_MODEL),
        "conv_w": w(ks[7], (DEPTH, CONV_KERNEL, CONV_WIDTH), CONV_KERNEL),
        "conv_b": 0.02 * jax.random.normal(ks[8], (DEPTH, CONV_WIDTH), f32),
        "conv_ln_g": gain(ks[9], (DEPTH, CONV_WIDTH)),
        "conv_ln_b": 0.02 * jax.random.normal(ks[10], (DEPTH, CONV_WIDTH), f32),
        "w_out": w(ks[11], (DEPTH, MIX_WIDTH, D_MODEL), MIX_WIDTH),
        "ffn2_norm": gain(ks[12], (DEPTH, D_MODEL)),
        "ffn2_wg": w(ks[13], (DEPTH, D_MODEL, D_FF), D_MODEL),
        "ffn2_wu": w(ks[14], (DEPTH, D_MODEL, D_FF), D_MODEL),
        "ffn2_wd": w(ks[15], (DEPTH, D_FF, D_MODEL), D_FF),
        "final_norm": gain(ks[16], (D_MODEL,)),
    }


def reference(x, positions, ffn1_norm, ffn1_wg, ffn1_wu, ffn1_wd, mix_norm, w_in, conv_w, conv_b,
              conv_ln_g, conv_ln_b, w_out, ffn2_norm, ffn2_wg, ffn2_wu, ffn2_wd, final_norm):
    s = x.shape[1]
    s_pad = ((s + PAD_MULT - 1) // PAD_MULT) * PAD_MULT
    pad = s_pad - s
    pos = jnp.pad(positions, ((0, 0), (0, pad)))
    half_ret = HEAD_DIM // 2
    ret_inv = RET_ROT_BASE ** (-jnp.linspace(0.0, 1.0, half_ret, dtype=jnp.float32))
    half_rope = ROPE_DIMS // 2
    rope_inv = ROPE_THETA ** (-jnp.arange(half_rope, dtype=jnp.float32) / half_rope)
    sizes = [RET_WIDTH] * 4 + [MOBA_WIDTH] * 3 + [CONV_WIDTH] * 2
    cuts = [int(c) for c in np.cumsum(sizes)[:-1]]

    for l in range(DEPTH):
        h = rms_norm(x, ffn1_norm[l])
        x = x + FFN_RES_WEIGHT * swiglu(h, ffn1_wg[l], ffn1_wu[l], ffn1_wd[l])

        h = rms_norm(x, mix_norm[l])
        z = jnp.pad(h @ w_in[l], ((0, 0), (0, pad), (0, 0)))
        rq, rk, rv, rg, mq, mk, mv, ca, cg = jnp.split(z, cuts, axis=-1)

        rq = rotary(to_heads(rq, RET_HEADS), pos, ret_inv)
        rk = rotary(to_heads(rk, RET_HEADS), pos, ret_inv)
        y_ret = retention(rq, rk, to_heads(rv, RET_HEADS)).astype(jnp.float32)
        y_ret = y_ret * lax.rsqrt(jnp.mean(y_ret * y_ret, -1, keepdims=True) + EPS)
        y_ret = jax.nn.silu(rg) * from_heads(y_ret).astype(rg.dtype)

        mq = rotary(to_heads(mq, MOBA_HEADS), pos, rope_inv)
        mk = rotary(to_heads(mk, MOBA_HEADS), pos, rope_inv)
        y_moba = from_heads(moba_attention(mq, mk, to_heads(mv, MOBA_HEADS)))

        y_conv = conv_module(ca, cg, conv_w[l], conv_b[l], conv_ln_g[l], conv_ln_b[l])

        mix = jnp.concatenate([y_ret, y_moba, y_conv], axis=-1)[:, :s]
        x = x + mix @ w_out[l]

        h = rms_norm(x, ffn2_norm[l])
        x = x + FFN_RES_WEIGHT * swiglu(h, ffn2_wg[l], ffn2_wu[l], ffn2_wd[l])

    return rms_norm(x, final_norm)
```

```python
import functools

import numpy as np
import jax
import jax.numpy as jnp
from jax import lax
from jax.experimental import pallas as pl
from jax.experimental.pallas import tpu as pltpu

HEAD_DIM = 64
RET_CHUNK = 256
RET_ROT_BASE = 10000.0
MOBA_BLOCK = 256
MOBA_TOPK = 3
ROPE_THETA = 500000.0
ROPE_DIMS = HEAD_DIM // 4
CONV_KERNEL = 31
FFN_RES_WEIGHT = 0.5
EPS = 1e-6

LANES = 128
CONV_HALO = 32
NEG_BIG = -1e30
VMEM_LIMIT = 56 * 1024 * 1024

F32 = jnp.float32
BF16 = jnp.bfloat16


def _dot(a, b):
    return jnp.dot(a, b, preferred_element_type=F32)


def _dot_nt(a, b):
    return lax.dot_general(a, b, (((1,), (1,)), ((), ())), preferred_element_type=F32)


def _dot_tn(a, b):
    return lax.dot_general(a, b, (((0,), (0,)), ((), ())), preferred_element_type=F32)


def _split_bf16(x):
    hi = x.astype(BF16)
    lo = (x - hi.astype(F32)).astype(BF16)
    return hi, lo


def _group_mean(x, ones_bf16):
    hi, lo = _split_bf16(x)
    return _dot(hi, ones_bf16) + _dot(lo, ones_bf16)


def _rms(x, g):
    return x * lax.rsqrt(jnp.mean(x * x, axis=-1, keepdims=True) + EPS) * g


def _silu(x):
    return x * jax.nn.sigmoid(x)


def _table_kernel(pos_ref, inv_ref, cr_ref, sr_ref, cm_ref, sm_ref):
    pos = pos_ref[...].astype(F32)
    ang_r = pos * inv_ref[0:1, :]
    ang_m = pos * inv_ref[2:3, :]
    cr_ref[...] = jnp.cos(ang_r)
    sr_ref[...] = jnp.sin(ang_r) * inv_ref[1:2, :]
    cm_ref[...] = jnp.cos(ang_m)
    sm_ref[...] = jnp.sin(ang_m) * inv_ref[3:4, :]


def _rotary_tables(pos_col, inv):
    t = pos_col.shape[0]
    tm = 1024
    tab = jax.ShapeDtypeStruct((t, LANES), F32)
    spec = pl.BlockSpec((tm, LANES), lambda i: (i, 0))
    return pl.pallas_call(
        _table_kernel,
        grid=(t // tm,),
        in_specs=[pl.BlockSpec((tm, 1), lambda i: (i, 0)), pl.BlockSpec((8, LANES), lambda i: (0, 0))],
        out_specs=[spec] * 4,
        out_shape=[tab] * 4,
        name="rotary_tables",
    )(pos_col, inv)


def _ffn_kernel(*refs, with_mix, with_final):
    refs = list(refs)
    x_ref = refs.pop(0)
    if with_mix:
        yr_ref, ym_ref, yc_ref, wo_ref = refs[:4]
        refs = refs[4:]
    g_ref, wg_ref, wu_ref, wd_ref = refs[:4]
    refs = refs[4:]
    if with_final:
        fg_ref = refs.pop(0)
    o_ref, h_ref = refs
    f = pl.program_id(1)

    @pl.when(f == 0)
    def _():
        x = x_ref[...]
        if with_mix:
            rw = yr_ref.shape[1]
            mw = ym_ref.shape[1]
            x = x + _dot(yr_ref[...], wo_ref[0:rw, :])
            x = x + _dot(ym_ref[...], wo_ref[rw:rw + mw, :])
            x = x + _dot(yc_ref[...], wo_ref[rw + mw:, :])
        o_ref[...] = x
        h_ref[...] = _rms(x, g_ref[...]).astype(BF16)

    h = h_ref[...]
    a = (_silu(_dot(h, wg_ref[...])) * _dot(h, wu_ref[...])).astype(BF16)
    o_ref[...] += FFN_RES_WEIGHT * _dot(a, wd_ref[...])

    if with_final:
        @pl.when(f == pl.num_programs(1) - 1)
        def _():
            o_ref[...] = _rms(o_ref[...], fg_ref[...])


def _ffn(x, g, wg, wu, wd, mix=None, final_g=None):
    t, d = x.shape
    dff = wg.shape[1]
    tm = 512
    tf = 1408
    with_mix = mix is not None
    with_final = final_g is not None
    row = lambda i, f: (i, 0)
    const = lambda i, f: (0, 0)
    args = [x]
    specs = [pl.BlockSpec((tm, d), row)]
    if with_mix:
        yr, ym, yc, wo = mix
        args += [yr, ym, yc, wo]
        specs += [pl.BlockSpec((tm, yr.shape[1]), row), pl.BlockSpec((tm, ym.shape[1]), row),
                  pl.BlockSpec((tm, yc.shape[1]), row), pl.BlockSpec(wo.shape, const)]
    args += [g, wg, wu, wd]
    specs += [pl.BlockSpec((1, d), const), pl.BlockSpec((d, tf), lambda i, f: (0, f)),
              pl.BlockSpec((d, tf), lambda i, f: (0, f)), pl.BlockSpec((tf, d), lambda i, f: (f, 0))]
    if with_final:
        args.append(final_g)
        specs.append(pl.BlockSpec((1, d), const))
    return pl.pallas_call(
        functools.partial(_ffn_kernel, with_mix=with_mix, with_final=with_final),
        grid=(t // tm, dff // tf),
        in_specs=specs,
        out_specs=pl.BlockSpec((tm, d), row),
        out_shape=jax.ShapeDtypeStruct((t, d), F32),
        scratch_shapes=[pltpu.VMEM((tm, d), BF16)],
        compiler_params=pltpu.CompilerParams(dimension_semantics=("parallel", "arbitrary"),
                                             vmem_limit_bytes=VMEM_LIMIT),
        name="ffn",
    )(*args)


def _swap_halves(z, half):
    n = z.shape[1]
    d = lax.broadcasted_iota(jnp.int32, z.shape, 1) % HEAD_DIM
    return jnp.where(d < half, pltpu.roll(z, n - half, 1), pltpu.roll(z, half, 1))


def _inproj_kernel(x_ref, g_ref, w_ref, cr_ref, sr_ref, cm_ref, sm_ref,
                   rq_ref, rk_ref, rv_ref, rg_ref, mq_ref, mk_ref, mv_ref, ca_ref, cg_ref, km_ref,
                   *, rw, mw, cw):
    h = _rms(x_ref[...], g_ref[...]).astype(BF16)
    scale = HEAD_DIM ** -0.5

    def proj(lo, width):
        return _dot(h, w_ref[:, lo:lo + width])

    def rot(z, c, s, half):
        reps = z.shape[1] // LANES
        c = jnp.concatenate([c] * reps, axis=1)
        s = jnp.concatenate([s] * reps, axis=1)
        return z * c + _swap_halves(z, half) * s

    cr, sr = cr_ref[...], sr_ref[...]
    cm, sm = cm_ref[...], sm_ref[...]
    off = 0
    rq_ref[...] = rot(proj(off, rw), cr, sr, HEAD_DIM // 2).astype(BF16)
    off += rw
    rk_ref[...] = (rot(proj(off, rw), cr, sr, HEAD_DIM // 2) * scale).astype(BF16)
    off += rw
    rv_ref[...] = proj(off, rw).astype(BF16)
    off += rw
    rg_ref[...] = proj(off, rw).astype(BF16)
    off += rw
    mq_ref[...] = rot(proj(off, mw), cm, sm, ROPE_DIMS // 2) * scale
    off += mw
    mk = rot(proj(off, mw), cm, sm, ROPE_DIMS // 2)
    mk_ref[...] = mk.astype(BF16)
    nb = mk.shape[0] // MOBA_BLOCK
    km_ref[0] = jnp.mean(mk.reshape(nb, MOBA_BLOCK, mw), axis=1)
    off += mw
    mv_ref[...] = proj(off, mw).astype(BF16)
    off += mw
    ca_ref[...] = proj(off, cw)
    off += cw
    cg_ref[...] = proj(off, cw)


def _inproj(x, g, w_in, tabs, rw, mw, cw):
    t, d = x.shape
    tm = 512
    row = lambda i: (i, 0)
    const = lambda i: (0, 0)
    nb = tm // MOBA_BLOCK
    out_shape = ([jax.ShapeDtypeStruct((t, rw), BF16)] * 4
                 + [jax.ShapeDtypeStruct((t, mw), F32)] + [jax.ShapeDtypeStruct((t, mw), BF16)] * 2
                 + [jax.ShapeDtypeStruct((t, cw), F32)] * 2
                 + [jax.ShapeDtypeStruct((t // tm, nb, mw), F32)])
    out_specs = ([pl.BlockSpec((tm, rw), row)] * 4 + [pl.BlockSpec((tm, mw), row)] * 3
                 + [pl.BlockSpec((tm, cw), row)] * 2 + [pl.BlockSpec((1, nb, mw), lambda i: (i, 0, 0))])
    tab_spec = pl.BlockSpec((tm, LANES), row)
    return pl.pallas_call(
        functools.partial(_inproj_kernel, rw=rw, mw=mw, cw=cw),
        grid=(t // tm,),
        in_specs=[pl.BlockSpec((tm, d), row), pl.BlockSpec((1, d), const), pl.BlockSpec(w_in.shape, const)]
        + [tab_spec] * 4,
        out_specs=out_specs,
        out_shape=out_shape,
        compiler_params=pltpu.CompilerParams(dimension_semantics=("parallel",), vmem_limit_bytes=VMEM_LIMIT),
        name="inproj",
    )(x, g, w_in, *tabs)


def _ret_kernel(q_ref, k_ref, v_ref, g_ref, dec_ref, wc_ref, ws_ref, gam_ref, ones_ref, o_ref, st_ref):
    c = pl.program_id(2)

    @pl.when(c == 0)
    def _():
        st_ref[...] = jnp.zeros_like(st_ref)

    q, k, v = q_ref[...], k_ref[...], v_ref[...]
    lane = lax.broadcasted_iota(jnp.int32, q.shape, 1)
    first = lane < HEAD_DIM
    zero = jnp.zeros_like(q)
    p0 = (_dot_nt(jnp.where(first, q, zero), k) * dec_ref[0]).astype(BF16)
    p1 = (_dot_nt(jnp.where(first, zero, q), k) * dec_ref[1]).astype(BF16)
    inner = jnp.where(first, _dot(p0, v), _dot(p1, v))

    st = st_ref[...]
    qc = (q.astype(F32) * wc_ref[0]).astype(BF16)
    y = inner + _dot(qc, st.astype(BF16))

    kw = (k.astype(F32) * ws_ref[0]).astype(BF16)
    r = lax.broadcasted_iota(jnp.int32, st.shape, 0) < HEAD_DIM
    cc = lax.broadcasted_iota(jnp.int32, st.shape, 1) < HEAD_DIM
    st_ref[...] = st * gam_ref[0] + jnp.where(r == cc, _dot_tn(kw, v), 0.0)

    ms = _group_mean(y * y, ones_ref[...])
    yn = y * lax.rsqrt(ms + EPS)
    o_ref[...] = (_silu(g_ref[...].astype(F32)) * yn).astype(BF16)


def _retention(rq, rk, rv, rg, consts, batch, seq):
    dec, wc, ws, gam, ones = consts
    t, rw = rq.shape
    c = RET_CHUNK
    nc = seq // c
    pairs = rw // LANES
    blk = pl.BlockSpec((c, LANES), lambda b, p, i: (b * nc + i, p))
    per_pair = lambda shape: pl.BlockSpec((1,) + shape, lambda b, p, i: (p, 0, 0))
    return pl.pallas_call(
        _ret_kernel,
        grid=(batch, pairs, nc),
        in_specs=[blk, blk, blk, blk,
                  pl.BlockSpec((2, c, c), lambda b, p, i: (p, 0, 0)),
                  per_pair((c, LANES)), per_pair((c, LANES)), per_pair((LANES, LANES)),
                  pl.BlockSpec((LANES, LANES), lambda b, p, i: (0, 0))],
        out_specs=blk,
        out_shape=jax.ShapeDtypeStruct((t, rw), BF16),
        scratch_shapes=[pltpu.VMEM((LANES, LANES), F32)],
        compiler_params=pltpu.CompilerParams(dimension_semantics=("parallel", "parallel", "arbitrary")),
        name="retention",
    )(rq, rk, rv, rg, dec, wc, ws, gam, ones)


def _conv_kernel(a_ref, gt_ref, w_ref, b_ref, lg_ref, lb_ref, ones_ref, o_ref, u_ref):
    s = pl.program_id(1)
    ts = a_ref.shape[0]
    sub = 64

    @pl.when(s == 0)
    def _():
        u_ref[0:CONV_HALO, :] = jnp.zeros((CONV_HALO, u_ref.shape[1]), F32)

    u_ref[CONV_HALO:CONV_HALO + ts, :] = a_ref[...] * jax.nn.sigmoid(gt_ref[...])
    first_tap = CONV_HALO - (CONV_KERNEL - 1)
    for r in range(0, ts, sub):
        acc = jnp.broadcast_to(b_ref[...], (sub, u_ref.shape[1]))
        for kk in range(CONV_KERNEL):
            acc = acc + w_ref[kk:kk + 1, :] * u_ref[r + first_tap + kk:r + first_tap + kk + sub, :]
        mu = _group_mean(acc, ones_ref[...])
        dlt = acc - mu
        var = _group_mean(dlt * dlt, ones_ref[...])
        yn = dlt * lax.rsqrt(var + EPS) * lg_ref[...] + lb_ref[...]
        o_ref[r:r + sub, :] = _silu(yn).astype(BF16)
    u_ref[0:CONV_HALO, :] = u_ref[ts:ts + CONV_HALO, :]


def _conv(ca, cg, w, b, lg, lb, ones, batch, seq):
    t, cw = ca.shape
    ts = 512
    ns = seq // ts
    blk = pl.BlockSpec((ts, cw), lambda bb, i: (bb * ns + i, 0))
    const = lambda shape: pl.BlockSpec(shape, lambda bb, i: (0, 0))
    return pl.pallas_call(
        _conv_kernel,
        grid=(batch, ns),
        in_specs=[blk, blk, const(w.shape), const((1, cw)), const((1, cw)), const((1, cw)), const((cw, cw))],
        out_specs=blk,
        out_shape=jax.ShapeDtypeStruct((t, cw), BF16),
        scratch_shapes=[pltpu.VMEM((CONV_HALO + ts, cw), F32)],
        compiler_params=pltpu.CompilerParams(dimension_semantics=("parallel", "arbitrary")),
        name="conv",
    )(ca, cg, w, b, lg, lb, ones)


MOBA_ONES_ROWS = 16


def _moba_kernel(q_ref, k_ref, v_ref, km_ref, o_ref, vt_ref, acc_ref):
    i = pl.program_id(2)
    bs = MOBA_BLOCK
    nblk = k_ref.shape[0] // bs
    dv = v_ref.shape[1]

    @pl.when(i == 0)
    def _():
        vt_ref[dv:dv + MOBA_ONES_ROWS, :] = jnp.ones((MOBA_ONES_ROWS, vt_ref.shape[1]), BF16)

        def tr(n, carry):
            st = pl.multiple_of(n * bs, bs)
            vt_ref[0:dv, pl.ds(st, bs)] = v_ref[pl.ds(st, bs), :].astype(F32).T.astype(BF16)
            return carry
        lax.fori_loop(0, nblk, tr, 0)

    q = q_ref[...]
    lane = lax.broadcasted_iota(jnp.int32, q.shape, 1)
    first = lane < HEAD_DIM
    qs = jnp.concatenate([jnp.where(first, q, 0.0), jnp.where(first, 0.0, q)], axis=0)

    km = jnp.concatenate([km_ref[0], jnp.zeros((LANES - nblk, LANES), F32)], axis=0)
    qh, ql = _split_bf16(qs)
    kh, kl = _split_bf16(km)
    gate = _dot_nt(qh, kh) + _dot_nt(qh, kl) + _dot_nt(ql, kh)
    nidx = lax.broadcasted_iota(jnp.int32, gate.shape, 1).astype(F32)
    neg_inf = jnp.float32(-jnp.inf)
    gate = jnp.where(nidx < i.astype(F32), gate, neg_inf)
    bias = jnp.full(gate.shape, NEG_BIG, F32)
    for _ in range(MOBA_TOPK):
        mx = jnp.max(gate, axis=1, keepdims=True)
        pick = jnp.min(jnp.where(gate == mx, nidx, float(LANES)), axis=1, keepdims=True)
        hit = nidx == pick
        bias = jnp.where(hit & (mx > neg_inf), 0.0, bias)
        gate = jnp.where(hit, neg_inf, gate)
    qaug = jnp.concatenate([qs.astype(BF16), bias.astype(BF16)], axis=1)

    klane = lax.broadcasted_iota(jnp.int32, (bs, LANES), 1)

    def block(n, m, own):
        st = pl.multiple_of(n * bs, bs)
        kn = k_ref[pl.ds(st, bs), :]
        if own:
            sel = jnp.zeros((bs, LANES), BF16)
        else:
            sel = jnp.where(klane == n, 1.0, 0.0).astype(BF16)
        st_t = _dot_nt(jnp.concatenate([kn, sel], axis=1), qaug)
        if own:
            kpos = lax.broadcasted_iota(jnp.int32, st_t.shape, 0)
            qpos = lax.broadcasted_iota(jnp.int32, st_t.shape, 1) % bs
            st_t = jnp.where(kpos <= qpos, st_t, NEG_BIG)
        m_new = jnp.maximum(m, jnp.max(st_t, axis=0, keepdims=True))
        alpha = jnp.exp(m - m_new)
        p_t = jnp.exp(st_t - m_new).astype(BF16)
        pv = _dot(vt_ref[:, pl.ds(st, bs)], p_t)
        if own:
            acc_ref[...] = pv
        else:
            acc_ref[...] = acc_ref[...] * alpha + pv
        return m_new

    m0 = block(i, jnp.full((1, 2 * bs), NEG_BIG, F32), True)
    lax.fori_loop(0, i, lambda n, m: block(n, m, False), m0)

    acc = acc_ref[...]
    out_t = acc[0:dv, :] * (1.0 / acc[dv:dv + 1, :])
    out_t = jnp.concatenate([out_t[0:HEAD_DIM, 0:bs], out_t[HEAD_DIM:dv, bs:2 * bs]], axis=0)
    o_ref[...] = out_t.T.astype(BF16)


def _moba(mq, mk, mv, kmean, batch, seq):
    t, mw = mq.shape
    bs = MOBA_BLOCK
    nq = seq // bs
    pairs = mw // LANES
    qblk = pl.BlockSpec((bs, LANES), lambda b, p, i: (b * nq + i, p))
    kvblk = pl.BlockSpec((seq, LANES), lambda b, p, i: (b, p))
    return pl.pallas_call(
        _moba_kernel,
        grid=(batch, pairs, nq),
        in_specs=[qblk, kvblk, kvblk, pl.BlockSpec((1, nq, LANES), lambda b, p, i: (b, 0, p))],
        out_specs=qblk,
        out_shape=jax.ShapeDtypeStruct((t, mw), BF16),
        scratch_shapes=[pltpu.VMEM((LANES + MOBA_ONES_ROWS, seq), BF16),
                        pltpu.VMEM((LANES + MOBA_ONES_ROWS, 2 * bs), F32)],
        compiler_params=pltpu.CompilerParams(dimension_semantics=("parallel", "parallel", "arbitrary"),
                                             vmem_limit_bytes=VMEM_LIMIT),
        name="moba",
    )(mq, mk, mv, kmean)


def _lane_constants():
    d = np.arange(LANES) % HEAD_DIM
    half_ret = HEAD_DIM // 2
    ret_inv = RET_ROT_BASE ** (-jnp.linspace(0.0, 1.0, half_ret, dtype=F32))
    half_rope = ROPE_DIMS // 2
    rope_inv = ROPE_THETA ** (-jnp.arange(half_rope, dtype=F32) / half_rope)
    inv_r = ret_inv[d % half_ret]
    sgn_r = jnp.asarray(np.where(d < half_ret, -1.0, 1.0), F32)
    inv_m = jnp.where(jnp.asarray(d < ROPE_DIMS), rope_inv[d % half_rope], 0.0)
    sgn_m = jnp.asarray(np.where(d < half_rope, -1.0, 1.0), F32)
    zero = jnp.zeros((LANES,), F32)
    return jnp.stack([inv_r, sgn_r, inv_m, sgn_m, zero, zero, zero, zero])


def _retention_constants(heads):
    c = RET_CHUNK
    pairs = heads // 2
    log_gamma = jnp.log1p(-jnp.exp2(-5.0 - jnp.arange(heads, dtype=F32)))
    lg = log_gamma[:, None]
    idx = jnp.arange(c, dtype=F32)
    rel = idx[:, None] - idx[None, :]
    decay_in = jnp.where(rel >= 0, jnp.exp(lg[:, :, None] * jnp.maximum(rel, 0.0)), 0.0)
    w_state = jnp.exp(lg * (c - 1 - idx))
    w_cross = jnp.exp(lg * (idx + 1.0))
    chunk_decay = jnp.exp(log_gamma * c)

    def lanes(w):
        return jnp.repeat(w.reshape(pairs, 2, c).transpose(0, 2, 1), HEAD_DIM, axis=2)

    gam = jnp.repeat(chunk_decay.reshape(pairs, 1, 2), HEAD_DIM, axis=2)
    gam = jnp.broadcast_to(gam, (pairs, LANES, LANES))
    return decay_in, lanes(w_cross), lanes(w_state), gam


def _group_ones(width):
    g = np.arange(width) // HEAD_DIM
    return jnp.asarray((g[:, None] == g[None, :]) / HEAD_DIM, BF16)


def kernel(x, positions, ffn1_norm, ffn1_wg, ffn1_wu, ffn1_wd, mix_norm, w_in, conv_w, conv_b, conv_ln_g,
           conv_ln_b, w_out, ffn2_norm, ffn2_wg, ffn2_wu, ffn2_wd, final_norm):
    batch, seq, d = x.shape
    depth = w_in.shape[0]
    cw = conv_w.shape[2]
    rw = mw = (w_out.shape[1] - cw) // 2
    assert seq % MOBA_BLOCK == 0 and seq % RET_CHUNK == 0
    assert w_in.shape[2] == 4 * rw + 3 * mw + 2 * cw
    t = batch * seq

    tabs = _rotary_tables(positions.reshape(t, 1), _lane_constants())
    ret_consts = _retention_constants(rw // HEAD_DIM) + (_group_ones(LANES),)
    conv_ones = _group_ones(cw)

    xf = x.reshape(t, d)
    row = lambda v: v.reshape(1, -1)
    for l in range(depth):
        xf = _ffn(xf, row(ffn1_norm[l]), ffn1_wg[l].astype(BF16), ffn1_wu[l].astype(BF16),
                  ffn1_wd[l].astype(BF16))
        rq, rk, rv, rg, mq, mk, mv, ca, cg, km = _inproj(xf, row(mix_norm[l]), w_in[l].astype(BF16), tabs,
                                                         rw, mw, cw)
        y_ret = _retention(rq, rk, rv, rg, ret_consts, batch, seq)
        y_moba = _moba(mq, mk, mv, km.reshape(batch, seq // MOBA_BLOCK, mw), batch, seq)
        y_conv = _conv(ca, cg, conv_w[l], row(conv_b[l]), row(conv_ln_g[l]), row(conv_ln_b[l]), conv_ones,
                       batch, seq)
        xf = _ffn(xf, row(ffn2_norm[l]), ffn2_wg[l].astype(BF16), ffn2_wu[l].astype(BF16),
                  ffn2_wd[l].astype(BF16), mix=(y_ret, y_moba, y_conv, w_out[l].astype(BF16)),
                  final_g=row(final_norm) if l == depth - 1 else None)
    return xf.reshape(batch, seq, d)
```

```python
import functools

import numpy as np
import jax
import jax.numpy as jnp
from jax import lax
from jax.experimental import pallas as pl
from jax.experimental.pallas import tpu as pltpu

HEAD_DIM = 64
RET_CHUNK = 256
RET_ROT_BASE = 10000.0
MOBA_BLOCK = 256
MOBA_TOPK = 3
ROPE_THETA = 500000.0
ROPE_DIMS = HEAD_DIM // 4
CONV_KERNEL = 31
FFN_RES_WEIGHT = 0.5
EPS = 1e-6

LANES = 128
CONV_HALO = 32
NEG_BIG = -1e30
LOG2_E = 1.4426950408889634
VMEM_LIMIT = 56 * 1024 * 1024

F32 = jnp.float32
BF16 = jnp.bfloat16


def _dot(a, b):
    return jnp.dot(a, b, preferred_element_type=F32)


def _dot_nt(a, b):
    return lax.dot_general(a, b, (((1,), (1,)), ((), ())), preferred_element_type=F32)


def _dot_tn(a, b):
    return lax.dot_general(a, b, (((0,), (0,)), ((), ())), preferred_element_type=F32)


def _split_bf16(x):
    hi = x.astype(BF16)
    lo = (x - hi.astype(F32)).astype(BF16)
    return hi, lo


def _group_mean(x, ones_bf16):
    hi, lo = _split_bf16(x)
    return _dot(hi, ones_bf16) + _dot(lo, ones_bf16)


def _rms(x, g):
    return x * lax.rsqrt(jnp.mean(x * x, axis=-1, keepdims=True) + EPS) * g


def _silu(x):
    return x * jax.nn.sigmoid(x)


def _table_kernel(pos_ref, inv_ref, cr_ref, sr_ref, cm_ref, sm_ref):
    pos = pos_ref[...].astype(F32)
    ang_r = pos * inv_ref[0:1, :]
    ang_m = pos * inv_ref[2:3, :]
    cr_ref[...] = jnp.cos(ang_r)
    sr_ref[...] = jnp.sin(ang_r) * inv_ref[1:2, :]
    cm_ref[...] = jnp.cos(ang_m)
    sm_ref[...] = jnp.sin(ang_m) * inv_ref[3:4, :]


def _rotary_tables(pos_col, inv):
    t = pos_col.shape[0]
    tm = 1024
    tab = jax.ShapeDtypeStruct((t, LANES), F32)
    spec = pl.BlockSpec((tm, LANES), lambda i: (i, 0))
    return pl.pallas_call(
        _table_kernel,
        grid=(t // tm,),
        in_specs=[pl.BlockSpec((tm, 1), lambda i: (i, 0)), pl.BlockSpec((8, LANES), lambda i: (0, 0))],
        out_specs=[spec] * 4,
        out_shape=[tab] * 4,
        name="rotary_tables",
    )(pos_col, inv)


def _ffn_kernel(*refs, with_mix, with_final):
    refs = list(refs)
    x_ref = refs.pop(0)
    if with_mix:
        yr_ref, ym_ref, yc_ref, wo_ref = refs[:4]
        refs = refs[4:]
    g_ref, wg_ref, wu_ref, wd_ref = refs[:4]
    refs = refs[4:]
    if with_final:
        fg_ref = refs.pop(0)
    o_ref, h_ref = refs
    f = pl.program_id(1)

    @pl.when(f == 0)
    def _():
        x = x_ref[...]
        if with_mix:
            rw = yr_ref.shape[1]
            mw = ym_ref.shape[1]
            x = x + _dot(yr_ref[...], wo_ref[0:rw, :])
            x = x + _dot(ym_ref[...], wo_ref[rw:rw + mw, :])
            x = x + _dot(yc_ref[...], wo_ref[rw + mw:, :])
        o_ref[...] = x
        h_ref[...] = _rms(x, g_ref[...]).astype(BF16)

    h = h_ref[...]
    a = (_silu(_dot(h, wg_ref[...])) * _dot(h, wu_ref[...])).astype(BF16)
    o_ref[...] += FFN_RES_WEIGHT * _dot(a, wd_ref[...])

    if with_final:
        @pl.when(f == pl.num_programs(1) - 1)
        def _():
            o_ref[...] = _rms(o_ref[...], fg_ref[...])


def _ffn(x, g, wg, wu, wd, mix=None, final_g=None):
    t, d = x.shape
    dff = wg.shape[1]
    tm = 512
    tf = 1408
    with_mix = mix is not None
    with_final = final_g is not None
    row = lambda i, f: (i, 0)
    const = lambda i, f: (0, 0)
    args = [x]
    specs = [pl.BlockSpec((tm, d), row)]
    if with_mix:
        yr, ym, yc, wo = mix
        args += [yr, ym, yc, wo]
        specs += [pl.BlockSpec((tm, yr.shape[1]), row), pl.BlockSpec((tm, ym.shape[1]), row),
                  pl.BlockSpec((tm, yc.shape[1]), row), pl.BlockSpec(wo.shape, const)]
    args += [g, wg, wu, wd]
    specs += [pl.BlockSpec((1, d), const), pl.BlockSpec((d, tf), lambda i, f: (0, f)),
              pl.BlockSpec((d, tf), lambda i, f: (0, f)), pl.BlockSpec((tf, d), lambda i, f: (f, 0))]
    if with_final:
        args.append(final_g)
        specs.append(pl.BlockSpec((1, d), const))
    return pl.pallas_call(
        functools.partial(_ffn_kernel, with_mix=with_mix, with_final=with_final),
        grid=(t // tm, dff // tf),
        in_specs=specs,
        out_specs=pl.BlockSpec((tm, d), row),
        out_shape=jax.ShapeDtypeStruct((t, d), F32),
        scratch_shapes=[pltpu.VMEM((tm, d), BF16)],
        compiler_params=pltpu.CompilerParams(dimension_semantics=("parallel", "arbitrary"),
                                             vmem_limit_bytes=VMEM_LIMIT),
        name="ffn",
    )(*args)


def _swap_halves(z, half):
    n = z.shape[1]
    d = lax.broadcasted_iota(jnp.int32, z.shape, 1) % HEAD_DIM
    return jnp.where(d < half, pltpu.roll(z, n - half, 1), pltpu.roll(z, half, 1))


def _inproj_kernel(x_ref, g_ref, w_ref, cr_ref, sr_ref, cm_ref, sm_ref,
                   rq_ref, rk_ref, rv_ref, rg_ref, mq_ref, mk_ref, mv_ref, ca_ref, cg_ref, km_ref,
                   *, rw, mw, cw):
    h = _rms(x_ref[...], g_ref[...]).astype(BF16)
    scale = HEAD_DIM ** -0.5

    def proj(lo, width):
        return _dot(h, w_ref[:, lo:lo + width])

    def rot(z, c, s, half):
        reps = z.shape[1] // LANES
        c = jnp.concatenate([c] * reps, axis=1)
        s = jnp.concatenate([s] * reps, axis=1)
        return z * c + _swap_halves(z, half) * s

    cr, sr = cr_ref[...], sr_ref[...]
    cm, sm = cm_ref[...], sm_ref[...]
    off = 0
    rq_ref[...] = rot(proj(off, rw), cr, sr, HEAD_DIM // 2).astype(BF16)
    off += rw
    rk_ref[...] = (rot(proj(off, rw), cr, sr, HEAD_DIM // 2) * scale).astype(BF16)
    off += rw
    rv_ref[...] = proj(off, rw).astype(BF16)
    off += rw
    rg_ref[...] = proj(off, rw).astype(BF16)
    off += rw
    mq_ref[...] = rot(proj(off, mw), cm, sm, ROPE_DIMS // 2) * (scale * LOG2_E)
    off += mw
    mk = rot(proj(off, mw), cm, sm, ROPE_DIMS // 2)
    mk_ref[...] = mk.astype(BF16)
    nb = mk.shape[0] // MOBA_BLOCK
    km_ref[0] = jnp.mean(mk.reshape(nb, MOBA_BLOCK, mw), axis=1)
    off += mw
    mv_ref[...] = proj(off, mw).astype(BF16)
    off += mw
    ca_ref[...] = proj(off, cw)
    off += cw
    cg_ref[...] = proj(off, cw)


def _inproj(x, g, w_in, tabs, rw, mw, cw):
    t, d = x.shape
    tm = 512
    row = lambda i: (i, 0)
    const = lambda i: (0, 0)
    nb = tm // MOBA_BLOCK
    out_shape = ([jax.ShapeDtypeStruct((t, rw), BF16)] * 4
                 + [jax.ShapeDtypeStruct((t, mw), F32)] + [jax.ShapeDtypeStruct((t, mw), BF16)] * 2
                 + [jax.ShapeDtypeStruct((t, cw), F32)] * 2
                 + [jax.ShapeDtypeStruct((t // tm, nb, mw), F32)])
    out_specs = ([pl.BlockSpec((tm, rw), row)] * 4 + [pl.BlockSpec((tm, mw), row)] * 3
                 + [pl.BlockSpec((tm, cw), row)] * 2 + [pl.BlockSpec((1, nb, mw), lambda i: (i, 0, 0))])
    tab_spec = pl.BlockSpec((tm, LANES), row)
    return pl.pallas_call(
        functools.partial(_inproj_kernel, rw=rw, mw=mw, cw=cw),
        grid=(t // tm,),
        in_specs=[pl.BlockSpec((tm, d), row), pl.BlockSpec((1, d), const), pl.BlockSpec(w_in.shape, const)]
        + [tab_spec] * 4,
        out_specs=out_specs,
        out_shape=out_shape,
        compiler_params=pltpu.CompilerParams(dimension_semantics=("parallel",), vmem_limit_bytes=VMEM_LIMIT),
        name="inproj",
    )(x, g, w_in, *tabs)


def _ret_kernel(q_ref, k_ref, v_ref, g_ref, dec_ref, wc_ref, ws_ref, gam_ref, ones_ref, o_ref, st_ref):
    c = pl.program_id(2)

    @pl.when(c == 0)
    def _():
        st_ref[...] = jnp.zeros_like(st_ref)

    q, k, v = q_ref[...], k_ref[...], v_ref[...]
    lane = lax.broadcasted_iota(jnp.int32, q.shape, 1)
    first = lane < HEAD_DIM
    zero = jnp.zeros_like(q)
    p0 = (_dot_nt(jnp.where(first, q, zero), k) * dec_ref[0]).astype(BF16)
    p1 = (_dot_nt(jnp.where(first, zero, q), k) * dec_ref[1]).astype(BF16)
    inner = jnp.where(first, _dot(p0, v), _dot(p1, v))

    st = st_ref[...]
    qc = (q.astype(F32) * wc_ref[0]).astype(BF16)
    y = inner + _dot(qc, st.astype(BF16))

    kw = (k.astype(F32) * ws_ref[0]).astype(BF16)
    r = lax.broadcasted_iota(jnp.int32, st.shape, 0) < HEAD_DIM
    cc = lax.broadcasted_iota(jnp.int32, st.shape, 1) < HEAD_DIM
    st_ref[...] = st * gam_ref[0] + jnp.where(r == cc, _dot_tn(kw, v), 0.0)

    ms = _group_mean(y * y, ones_ref[...])
    yn = y * lax.rsqrt(ms + EPS)
    o_ref[...] = (_silu(g_ref[...].astype(F32)) * yn).astype(BF16)


def _retention(rq, rk, rv, rg, consts, batch, seq):
    dec, wc, ws, gam, ones = consts
    t, rw = rq.shape
    c = RET_CHUNK
    nc = seq // c
    pairs = rw // LANES
    blk = pl.BlockSpec((c, LANES), lambda b, p, i: (b * nc + i, p))
    per_pair = lambda shape: pl.BlockSpec((1,) + shape, lambda b, p, i: (p, 0, 0))
    return pl.pallas_call(
        _ret_kernel,
        grid=(batch, pairs, nc),
        in_specs=[blk, blk, blk, blk,
                  pl.BlockSpec((2, c, c), lambda b, p, i: (p, 0, 0)),
                  per_pair((c, LANES)), per_pair((c, LANES)), per_pair((LANES, LANES)),
                  pl.BlockSpec((LANES, LANES), lambda b, p, i: (0, 0))],
        out_specs=blk,
        out_shape=jax.ShapeDtypeStruct((t, rw), BF16),
        scratch_shapes=[pltpu.VMEM((LANES, LANES), F32)],
        compiler_params=pltpu.CompilerParams(dimension_semantics=("parallel", "parallel", "arbitrary")),
        name="retention",
    )(rq, rk, rv, rg, dec, wc, ws, gam, ones)


def _conv_kernel(a_ref, gt_ref, w_ref, b_ref, lg_ref, lb_ref, ones_ref, o_ref, u_ref):
    s = pl.program_id(1)
    ts = a_ref.shape[0]
    sub = 64

    @pl.when(s == 0)
    def _():
        u_ref[0:CONV_HALO, :] = jnp.zeros((CONV_HALO, u_ref.shape[1]), F32)

    u_ref[CONV_HALO:CONV_HALO + ts, :] = a_ref[...] * jax.nn.sigmoid(gt_ref[...])
    first_tap = CONV_HALO - (CONV_KERNEL - 1)
    for r in range(0, ts, sub):
        acc = jnp.broadcast_to(b_ref[...], (sub, u_ref.shape[1]))
        for kk in range(CONV_KERNEL):
            acc = acc + w_ref[kk:kk + 1, :] * u_ref[r + first_tap + kk:r + first_tap + kk + sub, :]
        mu = _group_mean(acc, ones_ref[...])
        dlt = acc - mu
        var = _group_mean(dlt * dlt, ones_ref[...])
        yn = dlt * lax.rsqrt(var + EPS) * lg_ref[...] + lb_ref[...]
        o_ref[r:r + sub, :] = _silu(yn).astype(BF16)
    u_ref[0:CONV_HALO, :] = u_ref[ts:ts + CONV_HALO, :]


def _conv(ca, cg, w, b, lg, lb, ones, batch, seq):
    t, cw = ca.shape
    ts = 512
    ns = seq // ts
    blk = pl.BlockSpec((ts, cw), lambda bb, i: (bb * ns + i, 0))
    const = lambda shape: pl.BlockSpec(shape, lambda bb, i: (0, 0))
    return pl.pallas_call(
        _conv_kernel,
        grid=(batch, ns),
        in_specs=[blk, blk, const(w.shape), const((1, cw)), const((1, cw)), const((1, cw)), const((cw, cw))],
        out_specs=blk,
        out_shape=jax.ShapeDtypeStruct((t, cw), BF16),
        scratch_shapes=[pltpu.VMEM((CONV_HALO + ts, cw), F32)],
        compiler_params=pltpu.CompilerParams(dimension_semantics=("parallel", "arbitrary")),
        name="conv",
    )(ca, cg, w, b, lg, lb, ones)


MOBA_ONES_ROWS = 16


def _moba_kernel(q_ref, k_ref, v_ref, km_ref, o_ref, vt_ref, acc_ref, s0_ref, s1_ref, p0_ref, p1_ref, al_ref):
    i = pl.program_id(2)
    bs = MOBA_BLOCK
    nblk = k_ref.shape[0] // bs
    dv = v_ref.shape[1]
    rows = 2 * bs

    @pl.when(i == 0)
    def _():
        vt_ref[dv:dv + MOBA_ONES_ROWS, :] = jnp.ones((MOBA_ONES_ROWS, vt_ref.shape[1]), BF16)

        def tr(n, carry):
            st = pl.multiple_of(n * bs, bs)
            vt_ref[0:dv, pl.ds(st, bs)] = v_ref[pl.ds(st, bs), :].astype(F32).T.astype(BF16)
            return carry
        lax.fori_loop(0, nblk, tr, 0)

    q_t = q_ref[...].T
    q_t = jnp.concatenate([q_t, q_t], axis=1)
    dim_first = lax.broadcasted_iota(jnp.int32, q_t.shape, 0) < HEAD_DIM
    col_first = lax.broadcasted_iota(jnp.int32, q_t.shape, 1) < bs
    qs_t = jnp.where(dim_first == col_first, q_t, 0.0)

    qh, ql = _split_bf16(qs_t)
    kh, kl = _split_bf16(km_ref[0])
    gate = _dot(kh, qh) + _dot(kh, ql) + _dot(kl, qh)
    nidx = lax.broadcasted_iota(jnp.int32, gate.shape, 0).astype(F32)
    neg_inf = jnp.float32(-jnp.inf)
    gate = jnp.where(nidx < i.astype(F32), gate, neg_inf)
    bias = jnp.full(gate.shape, NEG_BIG, F32)
    for _ in range(MOBA_TOPK):
        mx = jnp.max(gate, axis=0, keepdims=True)
        pick = jnp.min(jnp.where(gate == mx, nidx, float(LANES)), axis=0, keepdims=True)
        hit = nidx == pick
        bias = jnp.where(hit & (mx > neg_inf), 0.0, bias)
        gate = jnp.where(hit, neg_inf, gate)
    qaug_t = jnp.concatenate([qs_t.astype(BF16), bias.astype(BF16),
                              jnp.full((LANES - nblk, rows), NEG_BIG, BF16)], axis=0)

    klane = lax.broadcasted_iota(jnp.int32, (bs, LANES), 1)
    s_refs = (s0_ref, s1_ref)
    p_refs = (p0_ref, p1_ref)

    def stage_a(n, slot):
        st = pl.multiple_of(jnp.minimum(n, nblk - 1) * bs, bs)
        sel = jnp.where(klane == n, 1.0, 0.0).astype(BF16)
        s_refs[slot][...] = _dot(jnp.concatenate([k_ref[pl.ds(st, bs), :], sel], axis=1), qaug_t)

    def stage_b(slot, m):
        s = s_refs[slot][...]
        m_new = jnp.maximum(m, jnp.max(s, axis=0, keepdims=True))
        al_ref[slot:slot + 1, :] = jnp.exp2(m - m_new)
        p_refs[slot][...] = jnp.exp2(s - m_new).astype(BF16)
        return m_new

    def stage_c(n, slot):
        nv = jnp.where(n < 0, i, jnp.minimum(n, nblk - 1))
        st = pl.multiple_of(nv * bs, bs)
        pv = _dot(vt_ref[:, pl.ds(st, bs)], p_refs[slot][...])
        acc_ref[...] = acc_ref[...] * al_ref[slot:slot + 1, :] + pv

    own = pl.multiple_of(i * bs, bs)
    s_own = _dot(jnp.concatenate([k_ref[pl.ds(own, bs), :], jnp.zeros((bs, LANES), BF16)], axis=1), qaug_t)
    kpos = lax.broadcasted_iota(jnp.int32, s_own.shape, 0)
    qpos = lax.broadcasted_iota(jnp.int32, s_own.shape, 1) % bs
    s1_ref[...] = jnp.where(kpos <= qpos, s_own, NEG_BIG)
    p0_ref[...] = jnp.zeros(p0_ref.shape, BF16)
    al_ref[...] = jnp.ones(al_ref.shape, F32)
    acc_ref[...] = jnp.zeros(acc_ref.shape, F32)

    def two_steps(u, m):
        t = 2 * u
        stage_a(t, 0)
        m = stage_b(1, m)
        stage_c(t - 2, 0)
        stage_a(t + 1, 1)
        m = stage_b(0, m)
        stage_c(t - 1, 1)
        return m

    lax.fori_loop(0, (i + 3) // 2, two_steps, jnp.full((1, rows), NEG_BIG, F32))

    acc = acc_ref[...]
    out_t = acc[0:dv, :] * (1.0 / acc[dv:dv + 1, :])
    out_t = jnp.concatenate([out_t[0:HEAD_DIM, 0:bs], out_t[HEAD_DIM:dv, bs:rows]], axis=0)
    o_ref[...] = out_t.T.astype(BF16)


def _moba(mq, mk, mv, kmean, batch, seq):
    t, mw = mq.shape
    bs = MOBA_BLOCK
    nq = seq // bs
    pairs = mw // LANES
    qblk = pl.BlockSpec((bs, LANES), lambda b, p, i: (b * nq + i, p))
    kvblk = pl.BlockSpec((seq, LANES), lambda b, p, i: (b, p))
    return pl.pallas_call(
        _moba_kernel,
        grid=(batch, pairs, nq),
        in_specs=[qblk, kvblk, kvblk, pl.BlockSpec((1, nq, LANES), lambda b, p, i: (b, 0, p))],
        out_specs=qblk,
        out_shape=jax.ShapeDtypeStruct((t, mw), BF16),
        scratch_shapes=[pltpu.VMEM((LANES + MOBA_ONES_ROWS, seq), BF16),
                        pltpu.VMEM((LANES + MOBA_ONES_ROWS, 2 * bs), F32),
                        pltpu.VMEM((bs, 2 * bs), F32), pltpu.VMEM((bs, 2 * bs), F32),
                        pltpu.VMEM((bs, 2 * bs), BF16), pltpu.VMEM((bs, 2 * bs), BF16),
                        pltpu.VMEM((8, 2 * bs), F32)],
        compiler_params=pltpu.CompilerParams(dimension_semantics=("parallel", "parallel", "arbitrary"),
                                             vmem_limit_bytes=VMEM_LIMIT),
        name="moba",
    )(mq, mk, mv, kmean)


def _lane_constants():
    d = np.arange(LANES) % HEAD_DIM
    half_ret = HEAD_DIM // 2
    ret_inv = RET_ROT_BASE ** (-jnp.linspace(0.0, 1.0, half_ret, dtype=F32))
    half_rope = ROPE_DIMS // 2
    rope_inv = ROPE_THETA ** (-jnp.arange(half_rope, dtype=F32) / half_rope)
    inv_r = ret_inv[d % half_ret]
    sgn_r = jnp.asarray(np.where(d < half_ret, -1.0, 1.0), F32)
    inv_m = jnp.where(jnp.asarray(d < ROPE_DIMS), rope_inv[d % half_rope], 0.0)
    sgn_m = jnp.asarray(np.where(d < half_rope, -1.0, 1.0), F32)
    zero = jnp.zeros((LANES,), F32)
    return jnp.stack([inv_r, sgn_r, inv_m, sgn_m, zero, zero, zero, zero])


def _retention_constants(heads):
    c = RET_CHUNK
    pairs = heads // 2
    log_gamma = jnp.log1p(-jnp.exp2(-5.0 - jnp.arange(heads, dtype=F32)))
    lg = log_gamma[:, None]
    idx = jnp.arange(c, dtype=F32)
    rel = idx[:, None] - idx[None, :]
    decay_in = jnp.where(rel >= 0, jnp.exp(lg[:, :, None] * jnp.maximum(rel, 0.0)), 0.0)
    w_state = jnp.exp(lg * (c - 1 - idx))
    w_cross = jnp.exp(lg * (idx + 1.0))
    chunk_decay = jnp.exp(log_gamma * c)

    def lanes(w):
        return jnp.repeat(w.reshape(pairs, 2, c).transpose(0, 2, 1), HEAD_DIM, axis=2)

    gam = jnp.repeat(chunk_decay.reshape(pairs, 1, 2), HEAD_DIM, axis=2)
    gam = jnp.broadcast_to(gam, (pairs, LANES, LANES))
    return decay_in, lanes(w_cross), lanes(w_state), gam


def _group_ones(width):
    g = np.arange(width) // HEAD_DIM
    return jnp.asarray((g[:, None] == g[None, :]) / HEAD_DIM, BF16)


def kernel(x, positions, ffn1_norm, ffn1_wg, ffn1_wu, ffn1_wd, mix_norm, w_in, conv_w, conv_b, conv_ln_g,
           conv_ln_b, w_out, ffn2_norm, ffn2_wg, ffn2_wu, ffn2_wd, final_norm):
    batch, seq, d = x.shape
    depth = w_in.shape[0]
    cw = conv_w.shape[2]
    rw = mw = (w_out.shape[1] - cw) // 2
    assert seq % MOBA_BLOCK == 0 and seq % RET_CHUNK == 0
    assert w_in.shape[2] == 4 * rw + 3 * mw + 2 * cw
    t = batch * seq

    tabs = _rotary_tables(positions.reshape(t, 1), _lane_constants())
    ret_consts = _retention_constants(rw // HEAD_DIM) + (_group_ones(LANES),)
    conv_ones = _group_ones(cw)

    xf = x.reshape(t, d)
    row = lambda v: v.reshape(1, -1)
    for l in range(depth):
        xf = _ffn(xf, row(ffn1_norm[l]), ffn1_wg[l].astype(BF16), ffn1_wu[l].astype(BF16),
                  ffn1_wd[l].astype(BF16))
        rq, rk, rv, rg, mq, mk, mv, ca, cg, km = _inproj(xf, row(mix_norm[l]), w_in[l].astype(BF16), tabs,
                                                         rw, mw, cw)
        y_ret = _retention(rq, rk, rv, rg, ret_consts, batch, seq)
        y_moba = _moba(mq, mk, mv, km.reshape(batch, seq // MOBA_BLOCK, mw), batch, seq)
        y_conv = _conv(ca, cg, conv_w[l], row(conv_b[l]), row(conv_ln_g[l]), row(conv_ln_b[l]), conv_ones,
                       batch, seq)
        xf = _ffn(xf, row(ffn2_norm[l]), ffn2_wg[l].astype(BF16), ffn2_wu[l].astype(BF16),
                  ffn2_wd[l].astype(BF16), mix=(y_ret, y_moba, y_conv, w_out[l].astype(BF16)),
                  final_g=row(final_norm) if l == depth - 1 else None)
    return xf.reshape(batch, seq, d)
```

```python
import functools

import numpy as np
import jax
import jax.numpy as jnp
from jax import lax
from jax.experimental import pallas as pl
from jax.experimental.pallas import tpu as pltpu

HEAD_DIM = 64
RET_CHUNK = 256
RET_ROT_BASE = 10000.0
MOBA_BLOCK = 256
MOBA_TOPK = 3
ROPE_THETA = 500000.0
ROPE_DIMS = HEAD_DIM // 4
CONV_KERNEL = 31
FFN_RES_WEIGHT = 0.5
EPS = 1e-6

LANES = 128
CONV_HALO = 32
NEG_BIG = -1e30
LOG2_E = 1.4426950408889634
VMEM_LIMIT = 56 * 1024 * 1024

F32 = jnp.float32
BF16 = jnp.bfloat16


def _dot(a, b):
    return jnp.dot(a, b, preferred_element_type=F32)


def _dot_nt(a, b):
    return lax.dot_general(a, b, (((1,), (1,)), ((), ())), preferred_element_type=F32)


def _dot_tn(a, b):
    return lax.dot_general(a, b, (((0,), (0,)), ((), ())), preferred_element_type=F32)


def _split_bf16(x):
    hi = x.astype(BF16)
    lo = (x - hi.astype(F32)).astype(BF16)
    return hi, lo


def _group_mean(x, ones_bf16):
    hi, lo = _split_bf16(x)
    return _dot(hi, ones_bf16) + _dot(lo, ones_bf16)


def _rms(x, g):
    return x * lax.rsqrt(jnp.mean(x * x, axis=-1, keepdims=True) + EPS) * g


def _silu(x):
    return x * jax.nn.sigmoid(x)


def _table_kernel(pos_ref, inv_ref, cr_ref, sr_ref, cm_ref, sm_ref):
    pos = pos_ref[...].astype(F32)
    ang_r = pos * inv_ref[0:1, :]
    ang_m = pos * inv_ref[2:3, :]
    cr_ref[...] = jnp.cos(ang_r)
    sr_ref[...] = jnp.sin(ang_r) * inv_ref[1:2, :]
    cm_ref[...] = jnp.cos(ang_m)
    sm_ref[...] = jnp.sin(ang_m) * inv_ref[3:4, :]


def _rotary_tables(pos_col, inv):
    t = pos_col.shape[0]
    tm = 1024
    tab = jax.ShapeDtypeStruct((t, LANES), F32)
    spec = pl.BlockSpec((tm, LANES), lambda i: (i, 0))
    return pl.pallas_call(
        _table_kernel,
        grid=(t // tm,),
        in_specs=[pl.BlockSpec((tm, 1), lambda i: (i, 0)), pl.BlockSpec((8, LANES), lambda i: (0, 0))],
        out_specs=[spec] * 4,
        out_shape=[tab] * 4,
        name="rotary_tables",
    )(pos_col, inv)


def _ffn_kernel(*refs, with_mix, with_final):
    refs = list(refs)
    x_ref = refs.pop(0)
    if with_mix:
        yr_ref, ym_ref, yc_ref, wo_ref = refs[:4]
        refs = refs[4:]
    g_ref, wg_ref, wu_ref, wd_ref = refs[:4]
    refs = refs[4:]
    if with_final:
        fg_ref = refs.pop(0)
    (o_ref,) = refs

    for r in range(0, x_ref.shape[0], FFN_SUB_ROWS):
        rs = slice(r, r + FFN_SUB_ROWS)
        x = x_ref[rs, :]
        if with_mix:
            rw = yr_ref.shape[1]
            mw = ym_ref.shape[1]
            x = x + _dot(yr_ref[rs, :], wo_ref[0:rw, :])
            x = x + _dot(ym_ref[rs, :], wo_ref[rw:rw + mw, :])
            x = x + _dot(yc_ref[rs, :], wo_ref[rw + mw:, :])
        h = _rms(x, g_ref[...]).astype(BF16)
        a = (_silu(_dot(h, wg_ref[...])) * _dot(h, wu_ref[...])).astype(BF16)
        y = x + FFN_RES_WEIGHT * _dot(a, wd_ref[...])
        if with_final:
            y = _rms(y, fg_ref[...])
        o_ref[rs, :] = y


FFN_SUB_ROWS = 256


def _ffn(x, g, wg, wu, wd, mix=None, final_g=None):
    t, d = x.shape
    tm = 2 * FFN_SUB_ROWS
    with_mix = mix is not None
    with_final = final_g is not None
    row = lambda i: (i, 0)
    resident = lambda w: pl.BlockSpec(w.shape, lambda i: (0, 0), pipeline_mode=pl.Buffered(1))
    args = [x]
    specs = [pl.BlockSpec((tm, d), row)]
    if with_mix:
        yr, ym, yc, wo = mix
        args += [yr, ym, yc, wo]
        specs += [pl.BlockSpec((tm, yr.shape[1]), row), pl.BlockSpec((tm, ym.shape[1]), row),
                  pl.BlockSpec((tm, yc.shape[1]), row), resident(wo)]
    args += [g, wg, wu, wd]
    specs += [resident(g), resident(wg), resident(wu), resident(wd)]
    if with_final:
        args.append(final_g)
        specs.append(resident(final_g))
    return pl.pallas_call(
        functools.partial(_ffn_kernel, with_mix=with_mix, with_final=with_final),
        grid=(t // tm,),
        in_specs=specs,
        out_specs=pl.BlockSpec((tm, d), row),
        out_shape=jax.ShapeDtypeStruct((t, d), F32),
        compiler_params=pltpu.CompilerParams(dimension_semantics=("parallel",), vmem_limit_bytes=VMEM_LIMIT),
        name="ffn",
    )(*args)


def _swap_halves(z, half):
    n = z.shape[1]
    d = lax.broadcasted_iota(jnp.int32, z.shape, 1) % HEAD_DIM
    return jnp.where(d < half, pltpu.roll(z, n - half, 1), pltpu.roll(z, half, 1))


def _inproj_kernel(x_ref, g_ref, w_ref, cr_ref, sr_ref, cm_ref, sm_ref,
                   rq_ref, rk_ref, rv_ref, rg_ref, mq_ref, mk_ref, mv_ref, ca_ref, cg_ref, km_ref,
                   *, rw, mw, cw):
    h = _rms(x_ref[...], g_ref[...]).astype(BF16)
    scale = HEAD_DIM ** -0.5

    def proj(lo, width):
        return _dot(h, w_ref[:, lo:lo + width])

    def rot(z, c, s, half):
        reps = z.shape[1] // LANES
        c = jnp.concatenate([c] * reps, axis=1)
        s = jnp.concatenate([s] * reps, axis=1)
        return z * c + _swap_halves(z, half) * s

    cr, sr = cr_ref[...], sr_ref[...]
    cm, sm = cm_ref[...], sm_ref[...]
    off = 0
    rq_ref[...] = rot(proj(off, rw), cr, sr, HEAD_DIM // 2).astype(BF16)
    off += rw
    rk_ref[...] = (rot(proj(off, rw), cr, sr, HEAD_DIM // 2) * scale).astype(BF16)
    off += rw
    rv_ref[...] = proj(off, rw).astype(BF16)
    off += rw
    rg_ref[...] = proj(off, rw).astype(BF16)
    off += rw
    mq_ref[...] = rot(proj(off, mw), cm, sm, ROPE_DIMS // 2) * (scale * LOG2_E)
    off += mw
    mk = rot(proj(off, mw), cm, sm, ROPE_DIMS // 2)
    mk_ref[...] = mk.astype(BF16)
    nb = mk.shape[0] // MOBA_BLOCK
    km_ref[0] = jnp.mean(mk.reshape(nb, MOBA_BLOCK, mw), axis=1)
    off += mw
    mv_ref[...] = proj(off, mw).astype(BF16)
    off += mw
    ca_ref[...] = proj(off, cw)
    off += cw
    cg_ref[...] = proj(off, cw)


def _inproj(x, g, w_in, tabs, rw, mw, cw):
    t, d = x.shape
    tm = 512
    row = lambda i: (i, 0)
    const = lambda i: (0, 0)
    nb = tm // MOBA_BLOCK
    out_shape = ([jax.ShapeDtypeStruct((t, rw), BF16)] * 4
                 + [jax.ShapeDtypeStruct((t, mw), F32)] + [jax.ShapeDtypeStruct((t, mw), BF16)] * 2
                 + [jax.ShapeDtypeStruct((t, cw), F32)] * 2
                 + [jax.ShapeDtypeStruct((t // tm, nb, mw), F32)])
    out_specs = ([pl.BlockSpec((tm, rw), row)] * 4 + [pl.BlockSpec((tm, mw), row)] * 3
                 + [pl.BlockSpec((tm, cw), row)] * 2 + [pl.BlockSpec((1, nb, mw), lambda i: (i, 0, 0))])
    tab_spec = pl.BlockSpec((tm, LANES), row)
    return pl.pallas_call(
        functools.partial(_inproj_kernel, rw=rw, mw=mw, cw=cw),
        grid=(t // tm,),
        in_specs=[pl.BlockSpec((tm, d), row), pl.BlockSpec((1, d), const), pl.BlockSpec(w_in.shape, const)]
        + [tab_spec] * 4,
        out_specs=out_specs,
        out_shape=out_shape,
        compiler_params=pltpu.CompilerParams(dimension_semantics=("parallel",), vmem_limit_bytes=VMEM_LIMIT),
        name="inproj",
    )(x, g, w_in, *tabs)


def _ret_kernel(q_ref, k_ref, v_ref, g_ref, dec_ref, wc_ref, ws_ref, gam_ref, ones_ref, o_ref, st_ref):
    c = pl.program_id(2)

    @pl.when(c == 0)
    def _():
        st_ref[...] = jnp.zeros_like(st_ref)

    q, k, v = q_ref[...], k_ref[...], v_ref[...]
    lane = lax.broadcasted_iota(jnp.int32, q.shape, 1)
    first = lane < HEAD_DIM
    zero = jnp.zeros_like(q)
    p0 = (_dot_nt(jnp.where(first, q, zero), k) * dec_ref[0]).astype(BF16)
    p1 = (_dot_nt(jnp.where(first, zero, q), k) * dec_ref[1]).astype(BF16)
    inner = jnp.where(first, _dot(p0, v), _dot(p1, v))

    st = st_ref[...]
    qc = (q.astype(F32) * wc_ref[0]).astype(BF16)
    y = inner + _dot(qc, st.astype(BF16))

    kw = (k.astype(F32) * ws_ref[0]).astype(BF16)
    r = lax.broadcasted_iota(jnp.int32, st.shape, 0) < HEAD_DIM
    cc = lax.broadcasted_iota(jnp.int32, st.shape, 1) < HEAD_DIM
    st_ref[...] = st * gam_ref[0] + jnp.where(r == cc, _dot_tn(kw, v), 0.0)

    ms = _group_mean(y * y, ones_ref[...])
    yn = y * lax.rsqrt(ms + EPS)
    o_ref[...] = (_silu(g_ref[...].astype(F32)) * yn).astype(BF16)


def _retention(rq, rk, rv, rg, consts, batch, seq):
    dec, wc, ws, gam, ones = consts
    t, rw = rq.shape
    c = RET_CHUNK
    nc = seq // c
    pairs = rw // LANES
    blk = pl.BlockSpec((c, LANES), lambda b, p, i: (b * nc + i, p))
    per_pair = lambda shape: pl.BlockSpec((1,) + shape, lambda b, p, i: (p, 0, 0))
    return pl.pallas_call(
        _ret_kernel,
        grid=(batch, pairs, nc),
        in_specs=[blk, blk, blk, blk,
                  pl.BlockSpec((2, c, c), lambda b, p, i: (p, 0, 0)),
                  per_pair((c, LANES)), per_pair((c, LANES)), per_pair((LANES, LANES)),
                  pl.BlockSpec((LANES, LANES), lambda b, p, i: (0, 0))],
        out_specs=blk,
        out_shape=jax.ShapeDtypeStruct((t, rw), BF16),
        scratch_shapes=[pltpu.VMEM((LANES, LANES), F32)],
        compiler_params=pltpu.CompilerParams(dimension_semantics=("parallel", "parallel", "arbitrary")),
        name="retention",
    )(rq, rk, rv, rg, dec, wc, ws, gam, ones)


def _conv_kernel(a_ref, gt_ref, w_ref, b_ref, lg_ref, lb_ref, ones_ref, o_ref, u_ref):
    s = pl.program_id(1)
    ts = a_ref.shape[0]
    sub = 64

    @pl.when(s == 0)
    def _():
        u_ref[0:CONV_HALO, :] = jnp.zeros((CONV_HALO, u_ref.shape[1]), F32)

    u_ref[CONV_HALO:CONV_HALO + ts, :] = a_ref[...] * jax.nn.sigmoid(gt_ref[...])
    first_tap = CONV_HALO - (CONV_KERNEL - 1)
    for r in range(0, ts, sub):
        acc = jnp.broadcast_to(b_ref[...], (sub, u_ref.shape[1]))
        for kk in range(CONV_KERNEL):
            acc = acc + w_ref[kk:kk + 1, :] * u_ref[r + first_tap + kk:r + first_tap + kk + sub, :]
        mu = _group_mean(acc, ones_ref[...])
        dlt = acc - mu
        var = _group_mean(dlt * dlt, ones_ref[...])
        yn = dlt * lax.rsqrt(var + EPS) * lg_ref[...] + lb_ref[...]
        o_ref[r:r + sub, :] = _silu(yn).astype(BF16)
    u_ref[0:CONV_HALO, :] = u_ref[ts:ts + CONV_HALO, :]


def _conv(ca, cg, w, b, lg, lb, ones, batch, seq):
    t, cw = ca.shape
    ts = 512
    ns = seq // ts
    blk = pl.BlockSpec((ts, cw), lambda bb, i: (bb * ns + i, 0))
    const = lambda shape: pl.BlockSpec(shape, lambda bb, i: (0, 0))
    return pl.pallas_call(
        _conv_kernel,
        grid=(batch, ns),
        in_specs=[blk, blk, const(w.shape), const((1, cw)), const((1, cw)), const((1, cw)), const((cw, cw))],
        out_specs=blk,
        out_shape=jax.ShapeDtypeStruct((t, cw), BF16),
        scratch_shapes=[pltpu.VMEM((CONV_HALO + ts, cw), F32)],
        compiler_params=pltpu.CompilerParams(dimension_semantics=("parallel", "arbitrary")),
        name="conv",
    )(ca, cg, w, b, lg, lb, ones)


MOBA_ONES_ROWS = 16


def _moba_kernel(q_ref, k_ref, v_ref, km_ref, o_ref, vt_ref, qa_ref, acc_ref, s0_ref, s1_ref, p0_ref, p1_ref,
                 al_ref):
    i = pl.program_id(2)
    bs = MOBA_BLOCK
    nblk = k_ref.shape[0] // bs
    rows = 2 * bs
    hd = HEAD_DIM
    tiles = (i, nblk - 1 - i)

    @pl.when(i == 0)
    def _():
        for h in range(2):
            vt_ref[h, hd:hd + MOBA_ONES_ROWS, :] = jnp.ones((MOBA_ONES_ROWS, vt_ref.shape[2]), BF16)

        def tr(n, carry):
            st = pl.multiple_of(n * bs, bs)
            v_t = v_ref[pl.ds(st, bs), :].astype(F32).T.astype(BF16)
            vt_ref[0, 0:hd, pl.ds(st, bs)] = v_t[0:hd]
            vt_ref[1, 0:hd, pl.ds(st, bs)] = v_t[hd:2 * hd]
            return carry
        lax.fori_loop(0, nblk, tr, 0)

    s_refs = (s0_ref, s1_ref)
    p_refs = (p0_ref, p1_ref)
    neg_inf = jnp.float32(-jnp.inf)

    def pv_heads(st, p_t):
        return [_dot(vt_ref[h, :, pl.ds(st, bs)], p_t[:, h * bs:(h + 1) * bs]) for h in range(2)]

    for w in range(2):
        tile = tiles[w]
        own = pl.multiple_of(tile * bs, bs)
        q_t = q_ref[pl.ds(own, bs), :].T
        q_t = jnp.concatenate([q_t, q_t], axis=1)
        dim_first = lax.broadcasted_iota(jnp.int32, q_t.shape, 0) < hd
        col_first = lax.broadcasted_iota(jnp.int32, q_t.shape, 1) < bs
        qs_t = jnp.where(dim_first == col_first, q_t, 0.0)

        qh, ql = _split_bf16(qs_t)
        kh, kl = _split_bf16(km_ref[0])
        gate = _dot(kh, qh) + _dot(kh, ql) + _dot(kl, qh)
        nidx = lax.broadcasted_iota(jnp.int32, gate.shape, 0).astype(F32)
        gate = jnp.where(nidx < tile.astype(F32), gate, neg_inf)
        bias = jnp.full(gate.shape, NEG_BIG, F32)
        for _ in range(MOBA_TOPK):
            mx = jnp.max(gate, axis=0, keepdims=True)
            pick = jnp.min(jnp.where(gate == mx, nidx, float(LANES)), axis=0, keepdims=True)
            hit = nidx == pick
            bias = jnp.where(hit & (mx > neg_inf), 0.0, bias)
            gate = jnp.where(hit, neg_inf, gate)
        qaug_t = jnp.concatenate([qs_t.astype(BF16), bias.astype(BF16),
                                  jnp.full((LANES - nblk, rows), NEG_BIG, BF16)], axis=0)
        qa_ref[w] = qaug_t

    sel_row = lax.broadcasted_iota(jnp.int32, (16, LANES), 1)

    def step_of(u):
        if u < 2:
            return None, u, pl.multiple_of(tiles[u] * bs, bs)
        s = u - 2
        is_b = s >= i
        blk = jnp.where(is_b, s - i, s)
        return (is_b, blk), is_b.astype(jnp.int32), pl.multiple_of(blk * bs, bs)

    def stage_a(u, slot):
        past, w, st = step_of(u)
        if past is None:
            sel = jnp.zeros((bs, LANES), BF16)
        else:
            sel = jnp.tile(jnp.where(sel_row == past[1], 1.0, 0.0).astype(BF16), (bs // 16, 1))
        sc = _dot(jnp.concatenate([k_ref[pl.ds(st, bs), :], sel], axis=1), qa_ref[w])
        if past is None:
            kpos = lax.broadcasted_iota(jnp.int32, sc.shape, 0)
            qpos = lax.broadcasted_iota(jnp.int32, sc.shape, 1) % bs
            sc = jnp.where(kpos <= qpos, sc, NEG_BIG)
        s_refs[slot][...] = sc

    def stage_b(u, slot, m_pair):
        past, w, _ = step_of(u)
        sc = s_refs[slot][...]
        m_loc = jnp.max(sc, axis=0, keepdims=True)
        if past is None:
            m_new = m_loc
            m_pair = tuple(m_new if j == w else m_pair[j] for j in range(2))
        else:
            is_b = past[0]
            m_old = jnp.where(is_b, m_pair[1], m_pair[0])
            m_new = jnp.maximum(m_old, m_loc)
            al_ref[slot:slot + 1, :] = jnp.exp2(m_old - m_new)
            m_pair = (jnp.where(is_b, m_pair[0], m_new), jnp.where(is_b, m_new, m_pair[1]))
        p_refs[slot][...] = jnp.exp2(sc - m_new).astype(BF16)
        return m_pair

    def stage_c(u, slot):
        past, w, st = step_of(u)
        pv = pv_heads(st, p_refs[slot][...])
        if past is None:
            for h in range(2):
                acc_ref[w, h] = pv[h]
        else:
            al = al_ref[slot:slot + 1, :]
            for h in range(2):
                acc_ref[w, h] = acc_ref[w, h] * al[:, h * bs:(h + 1) * bs] + pv[h]

    m_pair = (None, None)
    steps = nblk + 1
    for t in range(steps + 2):
        if t < steps:
            stage_a(t, t % 2)
        if 1 <= t <= steps:
            m_pair = stage_b(t - 1, (t - 1) % 2, m_pair)
        if t >= 2:
            stage_c(t - 2, t % 2)

    for w in range(2):
        outs = []
        for h in range(2):
            acc = acc_ref[w, h]
            outs.append(acc[0:hd, :] * (1.0 / acc[hd:hd + 1, :]))
        own = pl.multiple_of(tiles[w] * bs, bs)
        o_ref[pl.ds(own, bs), :] = jnp.concatenate(outs, axis=0).T.astype(BF16)


def _moba(mq, mk, mv, kmean, batch, seq):
    t, mw = mq.shape
    bs = MOBA_BLOCK
    nq = seq // bs
    assert nq % 2 == 0
    pairs = mw // LANES
    seqblk = pl.BlockSpec((seq, LANES), lambda b, p, i: (b, p))
    return pl.pallas_call(
        _moba_kernel,
        grid=(batch, pairs, nq // 2),
        in_specs=[seqblk, seqblk, seqblk, pl.BlockSpec((1, nq, LANES), lambda b, p, i: (b, 0, p))],
        out_specs=seqblk,
        out_shape=jax.ShapeDtypeStruct((t, mw), BF16),
        scratch_shapes=[pltpu.VMEM((2, HEAD_DIM + MOBA_ONES_ROWS, seq), BF16),
                        pltpu.VMEM((2, 2 * LANES, 2 * bs), BF16),
                        pltpu.VMEM((2, 2, HEAD_DIM + MOBA_ONES_ROWS, bs), F32),
                        pltpu.VMEM((bs, 2 * bs), F32), pltpu.VMEM((bs, 2 * bs), F32),
                        pltpu.VMEM((bs, 2 * bs), BF16), pltpu.VMEM((bs, 2 * bs), BF16),
                        pltpu.VMEM((8, 2 * bs), F32)],
        compiler_params=pltpu.CompilerParams(dimension_semantics=("parallel", "parallel", "arbitrary"),
                                             vmem_limit_bytes=VMEM_LIMIT),
        name="moba",
    )(mq, mk, mv, kmean)


def _lane_constants():
    d = np.arange(LANES) % HEAD_DIM
    half_ret = HEAD_DIM // 2
    ret_inv = RET_ROT_BASE ** (-jnp.linspace(0.0, 1.0, half_ret, dtype=F32))
    half_rope = ROPE_DIMS // 2
    rope_inv = ROPE_THETA ** (-jnp.arange(half_rope, dtype=F32) / half_rope)
    inv_r = ret_inv[d % half_ret]
    sgn_r = jnp.asarray(np.where(d < half_ret, -1.0, 1.0), F32)
    inv_m = jnp.where(jnp.asarray(d < ROPE_DIMS), rope_inv[d % half_rope], 0.0)
    sgn_m = jnp.asarray(np.where(d < half_rope, -1.0, 1.0), F32)
    zero = jnp.zeros((LANES,), F32)
    return jnp.stack([inv_r, sgn_r, inv_m, sgn_m, zero, zero, zero, zero])


def _retention_constants(heads):
    c = RET_CHUNK
    pairs = heads // 2
    log_gamma = jnp.log1p(-jnp.exp2(-5.0 - jnp.arange(heads, dtype=F32)))
    lg = log_gamma[:, None]
    idx = jnp.arange(c, dtype=F32)
    rel = idx[:, None] - idx[None, :]
    decay_in = jnp.where(rel >= 0, jnp.exp(lg[:, :, None] * jnp.maximum(rel, 0.0)), 0.0)
    w_state = jnp.exp(lg * (c - 1 - idx))
    w_cross = jnp.exp(lg * (idx + 1.0))
    chunk_decay = jnp.exp(log_gamma * c)

    def lanes(w):
        return jnp.repeat(w.reshape(pairs, 2, c).transpose(0, 2, 1), HEAD_DIM, axis=2)

    gam = jnp.repeat(chunk_decay.reshape(pairs, 1, 2), HEAD_DIM, axis=2)
    gam = jnp.broadcast_to(gam, (pairs, LANES, LANES))
    return decay_in, lanes(w_cross), lanes(w_state), gam


def _group_ones(width):
    g = np.arange(width) // HEAD_DIM
    return jnp.asarray((g[:, None] == g[None, :]) / HEAD_DIM, BF16)


def kernel(x, positions, ffn1_norm, ffn1_wg, ffn1_wu, ffn1_wd, mix_norm, w_in, conv_w, conv_b, conv_ln_g,
           conv_ln_b, w_out, ffn2_norm, ffn2_wg, ffn2_wu, ffn2_wd, final_norm):
    batch, seq, d = x.shape
    depth = w_in.shape[0]
    cw = conv_w.shape[2]
    rw = mw = (w_out.shape[1] - cw) // 2
    assert seq % MOBA_BLOCK == 0 and seq % RET_CHUNK == 0
    assert w_in.shape[2] == 4 * rw + 3 * mw + 2 * cw
    t = batch * seq

    tabs = _rotary_tables(positions.reshape(t, 1), _lane_constants())
    ret_consts = _retention_constants(rw // HEAD_DIM) + (_group_ones(LANES),)
    conv_ones = _group_ones(cw)

    xf = x.reshape(t, d)
    row = lambda v: v.reshape(1, -1)
    for l in range(depth):
        xf = _ffn(xf, row(ffn1_norm[l]), ffn1_wg[l].astype(BF16), ffn1_wu[l].astype(BF16),
                  ffn1_wd[l].astype(BF16))
        rq, rk, rv, rg, mq, mk, mv, ca, cg, km = _inproj(xf, row(mix_norm[l]), w_in[l].astype(BF16), tabs,
                                                         rw, mw, cw)
        y_ret = _retention(rq, rk, rv, rg, ret_consts, batch, seq)
        y_moba = _moba(mq, mk, mv, km.reshape(batch, seq // MOBA_BLOCK, mw), batch, seq)
        y_conv = _conv(ca, cg, conv_w[l], row(conv_b[l]), row(conv_ln_g[l]), row(conv_ln_b[l]), conv_ones,
                       batch, seq)
        xf = _ffn(xf, row(ffn2_norm[l]), ffn2_wg[l].astype(BF16), ffn2_wu[l].astype(BF16),
                  ffn2_wd[l].astype(BF16), mix=(y_ret, y_moba, y_conv, w_out[l].astype(BF16)),
                  final_g=row(final_norm) if l == depth - 1 else None)
    return xf.reshape(batch, seq, d)
```

```python
import functools

import numpy as np
import jax
import jax.numpy as jnp
from jax import lax
from jax.experimental import pallas as pl
from jax.experimental.pallas import tpu as pltpu

HEAD_DIM = 64
RET_CHUNK = 256
RET_ROT_BASE = 10000.0
MOBA_BLOCK = 256
MOBA_TOPK = 3
ROPE_THETA = 500000.0
ROPE_DIMS = HEAD_DIM // 4
CONV_KERNEL = 31
FFN_RES_WEIGHT = 0.5
EPS = 1e-6

LANES = 128
SUBLANES = 8
CONV_HALO = 32
NEG_BIG = -1e30
LOG2_E = 1.4426950408889634
VMEM_LIMIT = 56 * 1024 * 1024

F32 = jnp.float32
BF16 = jnp.bfloat16


def _dot(a, b):
    return jnp.dot(a, b, preferred_element_type=F32)


def _dot_nt(a, b):
    return lax.dot_general(a, b, (((1,), (1,)), ((), ())), preferred_element_type=F32)


def _dot_tn(a, b):
    return lax.dot_general(a, b, (((0,), (0,)), ((), ())), preferred_element_type=F32)


def _split_bf16(x):
    hi = x.astype(BF16)
    lo = (x - hi.astype(F32)).astype(BF16)
    return hi, lo


def _group_mean(x, ones_bf16):
    hi, lo = _split_bf16(x)
    return _dot(hi, ones_bf16) + _dot(lo, ones_bf16)


def _rms(x, g):
    return x * lax.rsqrt(jnp.mean(x * x, axis=-1, keepdims=True) + EPS) * g


def _silu(x):
    return x * jax.nn.sigmoid(x)


def _table_kernel(pos_ref, inv_ref, cr_ref, sr_ref, cm_ref, sm_ref):
    pos = pos_ref[...].astype(F32)
    ang_r = pos * inv_ref[0:1, :]
    ang_m = pos * inv_ref[2:3, :]
    cr_ref[...] = jnp.cos(ang_r)
    sr_ref[...] = jnp.sin(ang_r) * inv_ref[1:2, :]
    cm_ref[...] = jnp.cos(ang_m)
    sm_ref[...] = jnp.sin(ang_m) * inv_ref[3:4, :]


def _rotary_tables(pos_col, inv):
    t = pos_col.shape[0]
    tm = 1024
    tab = jax.ShapeDtypeStruct((t, LANES), F32)
    spec = pl.BlockSpec((tm, LANES), lambda i: (i, 0))
    return pl.pallas_call(
        _table_kernel,
        grid=(t // tm,),
        in_specs=[pl.BlockSpec((tm, 1), lambda i: (i, 0)), pl.BlockSpec((8, LANES), lambda i: (0, 0))],
        out_specs=[spec] * 4,
        out_shape=[tab] * 4,
        name="rotary_tables",
    )(pos_col, inv)


def _ffn_kernel(*refs, with_mix, with_final):
    refs = list(refs)
    x_ref = refs.pop(0)
    if with_mix:
        yr_ref, ym_ref, yc_ref, wo_ref = refs[:4]
        refs = refs[4:]
    g_ref, wg_ref, wu_ref, wd_ref = refs[:4]
    refs = refs[4:]
    if with_final:
        fg_ref = refs.pop(0)
    (o_ref,) = refs

    for r in range(0, x_ref.shape[0], FFN_SUB_ROWS):
        rs = slice(r, r + FFN_SUB_ROWS)
        x = x_ref[rs, :]
        if with_mix:
            mixed = jnp.concatenate([yr_ref[rs, :], ym_ref[rs, :], yc_ref[rs, :]], axis=1)
            x = x + _dot(mixed, wo_ref[...])
        h = _rms(x, g_ref[...]).astype(BF16)
        a = (_silu(_dot(h, wg_ref[...])) * _dot(h, wu_ref[...])).astype(BF16)
        y = x + FFN_RES_WEIGHT * _dot(a, wd_ref[...])
        if with_final:
            y = _rms(y, fg_ref[...])
        o_ref[rs, :] = y


FFN_SUB_ROWS = 256


def _layer_resident(w, layer):
    return pl.BlockSpec((None,) + w.shape[1:], lambda i: (layer,) + (0,) * (w.ndim - 1),
                        pipeline_mode=pl.Buffered(1))


def _ffn(x, layer, g, wg, wu, wd, mix=None, final_g=None):
    t, d = x.shape
    tm = 2 * FFN_SUB_ROWS
    with_mix = mix is not None
    with_final = final_g is not None
    row = lambda i: (i, 0)
    resident = lambda w: _layer_resident(w, layer)
    args = [x]
    specs = [pl.BlockSpec((tm, d), row)]
    if with_mix:
        yr, ym, yc, wo = mix
        args += [yr, ym, yc, wo]
        specs += [pl.BlockSpec((tm, yr.shape[1]), row), pl.BlockSpec((tm, ym.shape[1]), row),
                  pl.BlockSpec((tm, yc.shape[1]), row), resident(wo)]
    args += [g, wg, wu, wd]
    specs += [resident(g), resident(wg), resident(wu), resident(wd)]
    if with_final:
        args.append(final_g)
        specs.append(pl.BlockSpec(final_g.shape, lambda i: (0, 0)))
    return pl.pallas_call(
        functools.partial(_ffn_kernel, with_mix=with_mix, with_final=with_final),
        grid=(t // tm,),
        in_specs=specs,
        out_specs=pl.BlockSpec((tm, d), row),
        out_shape=jax.ShapeDtypeStruct((t, d), F32),
        compiler_params=pltpu.CompilerParams(dimension_semantics=("parallel",), vmem_limit_bytes=VMEM_LIMIT),
        name="ffn",
    )(*args)


def _swap_halves(z, half):
    n = z.shape[1]
    d = lax.broadcasted_iota(jnp.int32, z.shape, 1) % HEAD_DIM
    return jnp.where(d < half, pltpu.roll(z, n - half, 1), pltpu.roll(z, half, 1))


def _inproj_kernel(x_ref, g_ref, w_ref, cr_ref, sr_ref, cm_ref, sm_ref,
                   rq_ref, rk_ref, rv_ref, rg_ref, mq_ref, mk_ref, mv_ref, ca_ref, cg_ref, km_ref,
                   *, rw, mw, cw):
    scale = HEAD_DIM ** -0.5

    def rot(z, c, s, half):
        reps = z.shape[1] // LANES
        c = jnp.concatenate([c] * reps, axis=1)
        s = jnp.concatenate([s] * reps, axis=1)
        return z * c + _swap_halves(z, half) * s

    for j in range(x_ref.shape[0] // MOBA_BLOCK):
        rs = slice(j * MOBA_BLOCK, (j + 1) * MOBA_BLOCK)
        z = _dot(_rms(x_ref[rs, :], g_ref[...]).astype(BF16), w_ref[...])
        cr, sr = cr_ref[rs, :], sr_ref[rs, :]
        cm, sm = cm_ref[rs, :], sm_ref[rs, :]
        off = 0
        rq_ref[rs, :] = rot(z[:, off:off + rw], cr, sr, HEAD_DIM // 2).astype(BF16)
        off += rw
        rk_ref[rs, :] = (rot(z[:, off:off + rw], cr, sr, HEAD_DIM // 2) * scale).astype(BF16)
        off += rw
        rv_ref[rs, :] = z[:, off:off + rw].astype(BF16)
        off += rw
        rg_ref[rs, :] = z[:, off:off + rw].astype(BF16)
        off += rw
        mq_ref[rs, :] = rot(z[:, off:off + mw], cm, sm, ROPE_DIMS // 2) * (scale * LOG2_E)
        off += mw
        mk = rot(z[:, off:off + mw], cm, sm, ROPE_DIMS // 2)
        mk_ref[rs, :] = mk.astype(BF16)
        km_ref[0, j:j + 1, :] = jnp.mean(mk, axis=0, keepdims=True)
        off += mw
        mv_ref[rs, :] = z[:, off:off + mw].astype(BF16)
        off += mw
        ca_ref[rs, :] = z[:, off:off + cw]
        off += cw
        cg_ref[rs, :] = z[:, off:off + cw]


def _inproj(x, layer, g, w_in, tabs, rw, mw, cw):
    t, d = x.shape
    tm = 512
    row = lambda i: (i, 0)
    nb = tm // MOBA_BLOCK
    out_shape = ([jax.ShapeDtypeStruct((t, rw), BF16)] * 4
                 + [jax.ShapeDtypeStruct((t, mw), F32)] + [jax.ShapeDtypeStruct((t, mw), BF16)] * 2
                 + [jax.ShapeDtypeStruct((t, cw), F32)] * 2
                 + [jax.ShapeDtypeStruct((t // tm, nb, mw), F32)])
    out_specs = ([pl.BlockSpec((tm, rw), row)] * 4 + [pl.BlockSpec((tm, mw), row)] * 3
                 + [pl.BlockSpec((tm, cw), row)] * 2 + [pl.BlockSpec((1, nb, mw), lambda i: (i, 0, 0))])
    tab_spec = pl.BlockSpec((tm, LANES), row)
    return pl.pallas_call(
        functools.partial(_inproj_kernel, rw=rw, mw=mw, cw=cw),
        grid=(t // tm,),
        in_specs=[pl.BlockSpec((tm, d), row), _layer_resident(g, layer), _layer_resident(w_in, layer)]
        + [tab_spec] * 4,
        out_specs=out_specs,
        out_shape=out_shape,
        compiler_params=pltpu.CompilerParams(dimension_semantics=("parallel",), vmem_limit_bytes=VMEM_LIMIT),
        name="inproj",
    )(x, g, w_in, *tabs)


def _ret_kernel(q_ref, k_ref, v_ref, g_ref, dec_ref, wc_ref, ws_ref, gam_ref, ones_ref, o_ref, st_ref):
    batch = q_ref.shape[0]
    pairs = q_ref.shape[2] // LANES

    @pl.when(pl.program_id(0) == 0)
    def _():
        st_ref[...] = jnp.zeros_like(st_ref)

    for b in range(batch):
        for p in range(pairs):
            ls = slice(p * LANES, (p + 1) * LANES)
            q, k, v = q_ref[b, :, ls], k_ref[b, :, ls], v_ref[b, :, ls]
            lane = lax.broadcasted_iota(jnp.int32, q.shape, 1)
            first = lane < HEAD_DIM
            zero = jnp.zeros_like(q)
            p0 = (_dot_nt(jnp.where(first, q, zero), k) * dec_ref[2 * p]).astype(BF16)
            p1 = (_dot_nt(jnp.where(first, zero, q), k) * dec_ref[2 * p + 1]).astype(BF16)
            inner = jnp.where(first, _dot(p0, v), _dot(p1, v))

            ch = b * pairs + p
            st = st_ref[ch]
            qc = (q.astype(F32) * wc_ref[p]).astype(BF16)
            y = inner + _dot(qc, st.astype(BF16))

            kw = (k.astype(F32) * ws_ref[p]).astype(BF16)
            r = lax.broadcasted_iota(jnp.int32, st.shape, 0) < HEAD_DIM
            cc = lax.broadcasted_iota(jnp.int32, st.shape, 1) < HEAD_DIM
            st_ref[ch] = st * gam_ref[p] + jnp.where(r == cc, _dot_tn(kw, v), 0.0)

            ms = _group_mean(y * y, ones_ref[...])
            yn = y * lax.rsqrt(ms + EPS)
            o_ref[b, :, ls] = (_silu(g_ref[b, :, ls].astype(F32)) * yn).astype(BF16)


def _retention(rq, rk, rv, rg, consts, batch, seq):
    dec, wc, ws, gam, ones = consts
    t, rw = rq.shape
    c = RET_CHUNK
    pairs = rw // LANES
    blk = pl.BlockSpec((batch, c, rw), lambda i: (0, i, 0))
    whole = lambda a: pl.BlockSpec(a.shape, lambda i: (0,) * a.ndim)
    by_batch = lambda a: a.reshape(batch, seq, rw)
    out = pl.pallas_call(
        _ret_kernel,
        grid=(seq // c,),
        in_specs=[blk, blk, blk, blk, whole(dec), whole(wc), whole(ws), whole(gam), whole(ones)],
        out_specs=blk,
        out_shape=jax.ShapeDtypeStruct((batch, seq, rw), BF16),
        scratch_shapes=[pltpu.VMEM((batch * pairs, LANES, LANES), F32)],
        compiler_params=pltpu.CompilerParams(dimension_semantics=("arbitrary",)),
        name="retention",
    )(by_batch(rq), by_batch(rk), by_batch(rv), by_batch(rg), dec, wc, ws, gam, ones)
    return out.reshape(t, rw)


def _conv_kernel(a_ref, gt_ref, w_ref, b_ref, lg_ref, lb_ref, ones_ref, o_ref, u_ref, sh_ref):
    s = pl.program_id(1)
    ts = a_ref.shape[0]
    sub = 64

    @pl.when(s == 0)
    def _():
        u_ref[0:CONV_HALO, :] = jnp.zeros((CONV_HALO, u_ref.shape[1]), F32)

    u_ref[CONV_HALO:CONV_HALO + ts, :] = a_ref[...] * jax.nn.sigmoid(gt_ref[...])
    span = sh_ref.shape[1]
    for res in range(1, SUBLANES):
        sh_ref[res - 1] = u_ref[res:res + span, :]
    first_tap = CONV_HALO - (CONV_KERNEL - 1)

    def tap_rows(kk, r):
        res, base = (first_tap + kk) % SUBLANES, (first_tap + kk) // SUBLANES * SUBLANES
        if res == 0:
            return u_ref[r + base:r + base + sub, :]
        return sh_ref[res - 1, r + base:r + base + sub, :]

    for r in range(0, ts, sub):
        acc = jnp.broadcast_to(b_ref[...], (sub, u_ref.shape[1]))
        for kk in range(CONV_KERNEL):
            acc = acc + w_ref[kk:kk + 1, :] * tap_rows(kk, r)
        mu = _group_mean(acc, ones_ref[...])
        dlt = acc - mu
        var = _group_mean(dlt * dlt, ones_ref[...])
        yn = dlt * lax.rsqrt(var + EPS) * lg_ref[...] + lb_ref[...]
        o_ref[r:r + sub, :] = _silu(yn).astype(BF16)
    u_ref[0:CONV_HALO, :] = u_ref[ts:ts + CONV_HALO, :]


def _conv(ca, cg, w, b, lg, lb, ones, batch, seq):
    t, cw = ca.shape
    ts = 512
    ns = seq // ts
    blk = pl.BlockSpec((ts, cw), lambda bb, i: (bb * ns + i, 0))
    const = lambda shape: pl.BlockSpec(shape, lambda bb, i: (0, 0))
    return pl.pallas_call(
        _conv_kernel,
        grid=(batch, ns),
        in_specs=[blk, blk, const(w.shape), const((1, cw)), const((1, cw)), const((1, cw)), const((cw, cw))],
        out_specs=blk,
        out_shape=jax.ShapeDtypeStruct((t, cw), BF16),
        scratch_shapes=[pltpu.VMEM((CONV_HALO + ts, cw), F32),
                        pltpu.VMEM((SUBLANES - 1, ts + CONV_HALO - SUBLANES, cw), F32)],
        compiler_params=pltpu.CompilerParams(dimension_semantics=("parallel", "arbitrary")),
        name="conv",
    )(ca, cg, w, b, lg, lb, ones)


MOBA_ONES_ROWS = 16


def _moba_kernel(q_ref, k_ref, v_ref, km_ref, o_ref, vt_ref, qa_ref, acc_ref, s0_ref, s1_ref, p0_ref, p1_ref,
                 al_ref):
    i = pl.program_id(2)
    bs = MOBA_BLOCK
    nblk = k_ref.shape[0] // bs
    rows = 2 * bs
    hd = HEAD_DIM
    tiles = (nblk - 1 - i, i)

    @pl.when(i == 0)
    def _():
        for h in range(2):
            vt_ref[h, hd:hd + MOBA_ONES_ROWS, :] = jnp.ones((MOBA_ONES_ROWS, vt_ref.shape[2]), BF16)

        def tr(n, carry):
            st = pl.multiple_of(n * bs, bs)
            v_t = v_ref[pl.ds(st, bs), :].astype(F32).T.astype(BF16)
            vt_ref[0, 0:hd, pl.ds(st, bs)] = v_t[0:hd]
            vt_ref[1, 0:hd, pl.ds(st, bs)] = v_t[hd:2 * hd]
            return carry
        lax.fori_loop(0, nblk, tr, 0)

    s_refs = (s0_ref, s1_ref)
    p_refs = (p0_ref, p1_ref)
    neg_inf = jnp.float32(-jnp.inf)

    def pv_heads(st, p_t):
        return [_dot(vt_ref[h, :, pl.ds(st, bs)], p_t[:, h * bs:(h + 1) * bs]) for h in range(2)]

    def prep(w):
        tile = tiles[w]
        own = pl.multiple_of(tile * bs, bs)
        q_t = q_ref[pl.ds(own, bs), :].T
        q_t = jnp.concatenate([q_t, q_t], axis=1)
        dim_first = lax.broadcasted_iota(jnp.int32, q_t.shape, 0) < hd
        col_first = lax.broadcasted_iota(jnp.int32, q_t.shape, 1) < bs
        qs_t = jnp.where(dim_first == col_first, q_t, 0.0)

        qh, ql = _split_bf16(qs_t)
        kh, kl = _split_bf16(km_ref[0])
        gate = _dot(kh, qh) + _dot(kh, ql) + _dot(kl, qh)
        nidx = lax.broadcasted_iota(jnp.int32, gate.shape, 0).astype(F32)
        gate = jnp.where(nidx < tile.astype(F32), gate, neg_inf)
        bias = jnp.full(gate.shape, NEG_BIG, F32)
        for _ in range(MOBA_TOPK):
            mx = jnp.max(gate, axis=0, keepdims=True)
            pick = jnp.min(jnp.where(gate == mx, nidx, float(LANES)), axis=0, keepdims=True)
            hit = nidx == pick
            bias = jnp.where(hit & (mx > neg_inf), 0.0, bias)
            gate = jnp.where(hit, neg_inf, gate)
        qaug_t = jnp.concatenate([qs_t.astype(BF16), bias.astype(BF16),
                                  jnp.full((LANES - nblk, rows), NEG_BIG, BF16)], axis=0)
        qa_ref[w] = qaug_t

    n_long = tiles[0]
    half = nblk // 2
    seq = [("own", 0)] + [("past", s) for s in range(half)] + [("own", 1)] + [("past", s) for s in range(half, nblk - 1)]
    sel_row = lax.broadcasted_iota(jnp.int32, (16, LANES), 1)

    def step_of(step):
        kind, v = step
        if kind == "own":
            return True, None, v, None, pl.multiple_of(tiles[v] * bs, bs)
        if v < half:
            return False, False, 0, v, v * bs
        is_b = v >= n_long
        blk = jnp.where(is_b, v - n_long, v)
        return False, is_b, is_b.astype(jnp.int32), blk, pl.multiple_of(blk * bs, bs)

    def stage_a(step, slot):
        own, _, w, blk, st = step_of(step)
        if own:
            sel = jnp.zeros((bs, LANES), BF16)
        else:
            sel = jnp.tile(jnp.where(sel_row == blk, 1.0, 0.0).astype(BF16), (bs // 16, 1))
        sc = _dot(jnp.concatenate([k_ref[pl.ds(st, bs), :], sel], axis=1), qa_ref[w])
        if own:
            kpos = lax.broadcasted_iota(jnp.int32, sc.shape, 0)
            qpos = lax.broadcasted_iota(jnp.int32, sc.shape, 1) % bs
            sc = jnp.where(kpos <= qpos, sc, NEG_BIG)
        s_refs[slot][...] = sc

    def stage_b(step, slot, m_pair):
        own, is_b, w, _, _ = step_of(step)
        sc = s_refs[slot][...]
        m_loc = jnp.max(sc, axis=0, keepdims=True)
        if own:
            m_new = m_loc
            m_pair = tuple(m_new if j == w else m_pair[j] for j in range(2))
        else:
            if is_b is False:
                m_old = m_pair[0]
            else:
                m_old = jnp.where(is_b, m_pair[1], m_pair[0])
            m_new = jnp.maximum(m_old, m_loc)
            al_ref[slot:slot + 1, :] = jnp.exp2(m_old - m_new)
            if is_b is False:
                m_pair = (m_new, m_pair[1])
            else:
                m_pair = (jnp.where(is_b, m_pair[0], m_new), jnp.where(is_b, m_new, m_pair[1]))
        p_refs[slot][...] = jnp.exp2(sc - m_new).astype(BF16)
        return m_pair

    def stage_c(step, slot):
        own, _, w, _, st = step_of(step)
        pv = pv_heads(st, p_refs[slot][...])
        if own:
            for h in range(2):
                acc_ref[w, h] = pv[h]
        else:
            al = al_ref[slot:slot + 1, :]
            for h in range(2):
                acc_ref[w, h] = acc_ref[w, h] * al[:, h * bs:(h + 1) * bs] + pv[h]

    prep(0)
    prep(1)
    m_pair = (None, None)
    steps = len(seq)
    for t in range(steps + 2):
        if t < steps:
            stage_a(seq[t], t % 2)
        if 1 <= t <= steps:
            m_pair = stage_b(seq[t - 1], (t - 1) % 2, m_pair)
        if t >= 2:
            stage_c(seq[t - 2], t % 2)

    for w in range(2):
        outs = []
        for h in range(2):
            acc = acc_ref[w, h]
            outs.append(acc[0:hd, :] * (1.0 / acc[hd:hd + 1, :]))
        own = pl.multiple_of(tiles[w] * bs, bs)
        o_ref[pl.ds(own, bs), :] = jnp.concatenate(outs, axis=0).T.astype(BF16)


def _moba(mq, mk, mv, kmean, batch, seq):
    t, mw = mq.shape
    bs = MOBA_BLOCK
    nq = seq // bs
    assert nq % 2 == 0
    pairs = mw // LANES
    seqblk = pl.BlockSpec((seq, LANES), lambda b, p, i: (b, p))
    return pl.pallas_call(
        _moba_kernel,
        grid=(batch, pairs, nq // 2),
        in_specs=[seqblk, seqblk, seqblk, pl.BlockSpec((1, nq, LANES), lambda b, p, i: (b, 0, p))],
        out_specs=seqblk,
        out_shape=jax.ShapeDtypeStruct((t, mw), BF16),
        scratch_shapes=[pltpu.VMEM((2, HEAD_DIM + MOBA_ONES_ROWS, seq), BF16),
                        pltpu.VMEM((2, 2 * LANES, 2 * bs), BF16),
                        pltpu.VMEM((2, 2, HEAD_DIM + MOBA_ONES_ROWS, bs), F32),
                        pltpu.VMEM((bs, 2 * bs), F32), pltpu.VMEM((bs, 2 * bs), F32),
                        pltpu.VMEM((bs, 2 * bs), BF16), pltpu.VMEM((bs, 2 * bs), BF16),
                        pltpu.VMEM((8, 2 * bs), F32)],
        compiler_params=pltpu.CompilerParams(dimension_semantics=("parallel", "parallel", "arbitrary"),
                                             vmem_limit_bytes=VMEM_LIMIT),
        name="moba",
    )(mq, mk, mv, kmean)


def _lane_constants():
    d = np.arange(LANES) % HEAD_DIM
    half_ret = HEAD_DIM // 2
    ret_inv = RET_ROT_BASE ** (-jnp.linspace(0.0, 1.0, half_ret, dtype=F32))
    half_rope = ROPE_DIMS // 2
    rope_inv = ROPE_THETA ** (-jnp.arange(half_rope, dtype=F32) / half_rope)
    inv_r = ret_inv[d % half_ret]
    sgn_r = jnp.asarray(np.where(d < half_ret, -1.0, 1.0), F32)
    inv_m = jnp.where(jnp.asarray(d < ROPE_DIMS), rope_inv[d % half_rope], 0.0)
    sgn_m = jnp.asarray(np.where(d < half_rope, -1.0, 1.0), F32)
    zero = jnp.zeros((LANES,), F32)
    return jnp.stack([inv_r, sgn_r, inv_m, sgn_m, zero, zero, zero, zero])


def _retention_constants(heads):
    c = RET_CHUNK
    pairs = heads // 2
    log_gamma = jnp.log1p(-jnp.exp2(-5.0 - jnp.arange(heads, dtype=F32)))
    lg = log_gamma[:, None]
    idx = jnp.arange(c, dtype=F32)
    rel = idx[:, None] - idx[None, :]
    decay_in = jnp.where(rel >= 0, jnp.exp(lg[:, :, None] * jnp.maximum(rel, 0.0)), 0.0)
    w_state = jnp.exp(lg * (c - 1 - idx))
    w_cross = jnp.exp(lg * (idx + 1.0))
    chunk_decay = jnp.exp(log_gamma * c)

    def lanes(w):
        return jnp.repeat(w.reshape(pairs, 2, c).transpose(0, 2, 1), HEAD_DIM, axis=2)

    gam = jnp.repeat(chunk_decay.reshape(pairs, 1, 2), HEAD_DIM, axis=2)
    gam = jnp.broadcast_to(gam, (pairs, LANES, LANES))
    return decay_in, lanes(w_cross), lanes(w_state), gam


def _group_ones(width):
    g = np.arange(width) // HEAD_DIM
    return jnp.asarray((g[:, None] == g[None, :]) / HEAD_DIM, BF16)


def kernel(x, positions, ffn1_norm, ffn1_wg, ffn1_wu, ffn1_wd, mix_norm, w_in, conv_w, conv_b, conv_ln_g,
           conv_ln_b, w_out, ffn2_norm, ffn2_wg, ffn2_wu, ffn2_wd, final_norm):
    batch, seq, d = x.shape
    depth = w_in.shape[0]
    cw = conv_w.shape[2]
    rw = mw = (w_out.shape[1] - cw) // 2
    assert seq % MOBA_BLOCK == 0 and seq % RET_CHUNK == 0
    assert w_in.shape[2] == 4 * rw + 3 * mw + 2 * cw
    t = batch * seq

    tabs = _rotary_tables(positions.reshape(t, 1), _lane_constants())
    ret_consts = _retention_constants(rw // HEAD_DIM) + (_group_ones(LANES),)
    conv_ones = _group_ones(cw)

    xf = x.reshape(t, d)
    row = lambda v: v.reshape(1, -1)
    gains = lambda v: v.reshape(depth, 1, d)
    ffn1 = (gains(ffn1_norm), ffn1_wg.astype(BF16), ffn1_wu.astype(BF16), ffn1_wd.astype(BF16))
    ffn2 = (gains(ffn2_norm), ffn2_wg.astype(BF16), ffn2_wu.astype(BF16), ffn2_wd.astype(BF16))
    w_in_b, w_out_b, mix_g = w_in.astype(BF16), w_out.astype(BF16), gains(mix_norm)
    for l in range(depth):
        xf = _ffn(xf, l, *ffn1)
        rq, rk, rv, rg, mq, mk, mv, ca, cg, km = _inproj(xf, l, mix_g, w_in_b, tabs, rw, mw, cw)
        y_ret = _retention(rq, rk, rv, rg, ret_consts, batch, seq)
        y_moba = _moba(mq, mk, mv, km.reshape(batch, seq // MOBA_BLOCK, mw), batch, seq)
        y_conv = _conv(ca, cg, conv_w[l], row(conv_b[l]), row(conv_ln_g[l]), row(conv_ln_b[l]), conv_ones,
                       batch, seq)
        xf = _ffn(xf, l, *ffn2, mix=(y_ret, y_moba, y_conv, w_out_b),
                  final_g=row(final_norm) if l == depth - 1 else None)
    return xf.reshape(batch, seq, d)
```

```python
import functools

import numpy as np
import jax
import jax.numpy as jnp
from jax import lax
from jax.experimental import pallas as pl
from jax.experimental.pallas import tpu as pltpu

HEAD_DIM = 64
RET_CHUNK = 256
RET_ROT_BASE = 10000.0
MOBA_BLOCK = 256
MOBA_TOPK = 3
ROPE_THETA = 500000.0
ROPE_DIMS = HEAD_DIM // 4
CONV_KERNEL = 31
FFN_RES_WEIGHT = 0.5
EPS = 1e-6

LANES = 128
SUBLANES = 8
CONV_HALO = 32
NEG_BIG = -1e30
LOG2_E = 1.4426950408889634
VMEM_LIMIT = 56 * 1024 * 1024

F32 = jnp.float32
BF16 = jnp.bfloat16


def _dot(a, b):
    return jnp.dot(a, b, preferred_element_type=F32)


def _dot_nt(a, b):
    return lax.dot_general(a, b, (((1,), (1,)), ((), ())), preferred_element_type=F32)


def _dot_tn(a, b):
    return lax.dot_general(a, b, (((0,), (0,)), ((), ())), preferred_element_type=F32)


def _split_bf16(x):
    hi = x.astype(BF16)
    lo = (x - hi.astype(F32)).astype(BF16)
    return hi, lo


def _group_mean(x, ones_bf16):
    return _dot(x.astype(BF16), ones_bf16)


def _rms(x, g):
    return x * lax.rsqrt(jnp.mean(x * x, axis=-1, keepdims=True) + EPS) * g


def _silu(x):
    return x * jax.nn.sigmoid(x)


def _table_kernel(pos_ref, inv_ref, cr_ref, sr_ref, cm_ref, sm_ref):
    pos = pos_ref[...].astype(F32)
    ang_r = pos * inv_ref[0:1, :]
    ang_m = pos * inv_ref[2:3, :]
    cr_ref[...] = jnp.cos(ang_r)
    sr_ref[...] = jnp.sin(ang_r) * inv_ref[1:2, :]
    cm_ref[...] = jnp.cos(ang_m)
    sm_ref[...] = jnp.sin(ang_m) * inv_ref[3:4, :]


def _rotary_tables(pos_col, inv):
    t = pos_col.shape[0]
    tm = 1024
    tab = jax.ShapeDtypeStruct((t, LANES), F32)
    spec = pl.BlockSpec((tm, LANES), lambda i: (i, 0))
    return pl.pallas_call(
        _table_kernel,
        grid=(t // tm,),
        in_specs=[pl.BlockSpec((tm, 1), lambda i: (i, 0)), pl.BlockSpec((8, LANES), lambda i: (0, 0))],
        out_specs=[spec] * 4,
        out_shape=[tab] * 4,
        name="rotary_tables",
    )(pos_col, inv)


def _ffn_kernel(*refs, with_mix, with_final):
    refs = list(refs)
    x_ref = refs.pop(0)
    if with_mix:
        yr_ref, ym_ref, yc_ref, wo_ref = refs[:4]
        refs = refs[4:]
    g_ref, wg_ref, wu_ref, wd_ref = refs[:4]
    refs = refs[4:]
    if with_final:
        fg_ref = refs.pop(0)
    (o_ref,) = refs

    for r in range(0, x_ref.shape[0], FFN_SUB_ROWS):
        rs = slice(r, r + FFN_SUB_ROWS)
        x = x_ref[rs, :]
        if with_mix:
            mixed = jnp.concatenate([yr_ref[rs, :], ym_ref[rs, :], yc_ref[rs, :]], axis=1)
            x = x + _dot(mixed, wo_ref[...])
        h = _rms(x, g_ref[...]).astype(BF16)
        a = (_silu(_dot(h, wg_ref[...])) * _dot(h, wu_ref[...])).astype(BF16)
        y = x + FFN_RES_WEIGHT * _dot(a, wd_ref[...])
        if with_final:
            y = _rms(y, fg_ref[...])
        o_ref[rs, :] = y


FFN_SUB_ROWS = 256


def _layer_resident(w, layer):
    return pl.BlockSpec((None,) + w.shape[1:], lambda i: (layer,) + (0,) * (w.ndim - 1),
                        pipeline_mode=pl.Buffered(1))


def _ffn(x, layer, g, wg, wu, wd, mix=None, final_g=None):
    t, d = x.shape
    tm = 2 * FFN_SUB_ROWS
    with_mix = mix is not None
    with_final = final_g is not None
    row = lambda i: (i, 0)
    resident = lambda w: _layer_resident(w, layer)
    args = [x]
    specs = [pl.BlockSpec((tm, d), row)]
    if with_mix:
        yr, ym, yc, wo = mix
        args += [yr, ym, yc, wo]
        specs += [pl.BlockSpec((tm, yr.shape[1]), row), pl.BlockSpec((tm, ym.shape[1]), row),
                  pl.BlockSpec((tm, yc.shape[1]), row), resident(wo)]
    args += [g, wg, wu, wd]
    specs += [resident(g), resident(wg), resident(wu), resident(wd)]
    if with_final:
        args.append(final_g)
        specs.append(pl.BlockSpec(final_g.shape, lambda i: (0, 0)))
    return pl.pallas_call(
        functools.partial(_ffn_kernel, with_mix=with_mix, with_final=with_final),
        grid=(t // tm,),
        in_specs=specs,
        out_specs=pl.BlockSpec((tm, d), row),
        out_shape=jax.ShapeDtypeStruct((t, d), F32),
        compiler_params=pltpu.CompilerParams(dimension_semantics=("parallel",), vmem_limit_bytes=VMEM_LIMIT),
        name="ffn",
    )(*args)


def _swap_halves(z, half):
    n = z.shape[1]
    d = lax.broadcasted_iota(jnp.int32, z.shape, 1) % HEAD_DIM
    return jnp.where(d < half, pltpu.roll(z, n - half, 1), pltpu.roll(z, half, 1))


def _inproj_kernel(x_ref, g_ref, w_ref, cr_ref, sr_ref, cm_ref, sm_ref,
                   rq_ref, rk_ref, rv_ref, rg_ref, mq_ref, mk_ref, mv_ref, ca_ref, cg_ref, km_ref,
                   *, rw, mw, cw):
    scale = HEAD_DIM ** -0.5

    def rot(z, c, s, half):
        reps = z.shape[1] // LANES
        c = jnp.concatenate([c] * reps, axis=1)
        s = jnp.concatenate([s] * reps, axis=1)
        return z * c + _swap_halves(z, half) * s

    for j in range(x_ref.shape[0] // MOBA_BLOCK):
        rs = slice(j * MOBA_BLOCK, (j + 1) * MOBA_BLOCK)
        z = _dot(_rms(x_ref[rs, :], g_ref[...]).astype(BF16), w_ref[...])
        cr, sr = cr_ref[rs, :], sr_ref[rs, :]
        cm, sm = cm_ref[rs, :], sm_ref[rs, :]
        off = 0
        rq_ref[rs, :] = rot(z[:, off:off + rw], cr, sr, HEAD_DIM // 2).astype(BF16)
        off += rw
        rk_ref[rs, :] = (rot(z[:, off:off + rw], cr, sr, HEAD_DIM // 2) * scale).astype(BF16)
        off += rw
        rv_ref[rs, :] = z[:, off:off + rw].astype(BF16)
        off += rw
        rg_ref[rs, :] = z[:, off:off + rw].astype(BF16)
        off += rw
        mq_ref[rs, :] = rot(z[:, off:off + mw], cm, sm, ROPE_DIMS // 2) * (scale * LOG2_E)
        off += mw
        mk = rot(z[:, off:off + mw], cm, sm, ROPE_DIMS // 2)
        mk_ref[rs, :] = mk.astype(BF16)
        km_ref[0, j:j + 1, :] = jnp.mean(mk, axis=0, keepdims=True)
        off += mw
        mv_ref[rs, :] = z[:, off:off + mw].astype(BF16)
        off += mw
        ca_ref[rs, :] = z[:, off:off + cw]
        off += cw
        cg_ref[rs, :] = z[:, off:off + cw]


def _inproj(x, layer, g, w_in, tabs, rw, mw, cw):
    t, d = x.shape
    tm = 512
    row = lambda i: (i, 0)
    nb = tm // MOBA_BLOCK
    out_shape = ([jax.ShapeDtypeStruct((t, rw), BF16)] * 4
                 + [jax.ShapeDtypeStruct((t, mw), F32)] + [jax.ShapeDtypeStruct((t, mw), BF16)] * 2
                 + [jax.ShapeDtypeStruct((t, cw), F32)] * 2
                 + [jax.ShapeDtypeStruct((t // tm, nb, mw), F32)])
    out_specs = ([pl.BlockSpec((tm, rw), row)] * 4 + [pl.BlockSpec((tm, mw), row)] * 3
                 + [pl.BlockSpec((tm, cw), row)] * 2 + [pl.BlockSpec((1, nb, mw), lambda i: (i, 0, 0))])
    tab_spec = pl.BlockSpec((tm, LANES), row)
    return pl.pallas_call(
        functools.partial(_inproj_kernel, rw=rw, mw=mw, cw=cw),
        grid=(t // tm,),
        in_specs=[pl.BlockSpec((tm, d), row), _layer_resident(g, layer), _layer_resident(w_in, layer)]
        + [tab_spec] * 4,
        out_specs=out_specs,
        out_shape=out_shape,
        compiler_params=pltpu.CompilerParams(dimension_semantics=("parallel",), vmem_limit_bytes=VMEM_LIMIT),
        name="inproj",
    )(x, g, w_in, *tabs)


def _ret_kernel(q_ref, k_ref, v_ref, g_ref, dec_ref, wc_ref, ws_ref, gam_ref, ones_ref, o_ref, st_ref):
    batch = q_ref.shape[0]
    pairs = q_ref.shape[2] // LANES

    @pl.when(pl.program_id(0) == 0)
    def _():
        st_ref[...] = jnp.zeros_like(st_ref)

    for b in range(batch):
        for p in range(pairs):
            ls = slice(p * LANES, (p + 1) * LANES)
            q, k, v = q_ref[b, :, ls], k_ref[b, :, ls], v_ref[b, :, ls]
            lane = lax.broadcasted_iota(jnp.int32, q.shape, 1)
            first = lane < HEAD_DIM
            zero = jnp.zeros_like(q)
            p0 = (_dot_nt(jnp.where(first, q, zero), k) * dec_ref[2 * p]).astype(BF16)
            p1 = (_dot_nt(jnp.where(first, zero, q), k) * dec_ref[2 * p + 1]).astype(BF16)
            inner = jnp.where(first, _dot(p0, v), _dot(p1, v))

            ch = b * pairs + p
            st = st_ref[ch]
            qc = (q.astype(F32) * wc_ref[p]).astype(BF16)
            y = inner + _dot(qc, st.astype(BF16))

            kw = (k.astype(F32) * ws_ref[p]).astype(BF16)
            r = lax.broadcasted_iota(jnp.int32, st.shape, 0) < HEAD_DIM
            cc = lax.broadcasted_iota(jnp.int32, st.shape, 1) < HEAD_DIM
            st_ref[ch] = st * gam_ref[p] + jnp.where(r == cc, _dot_tn(kw, v), 0.0)

            ms = _group_mean(y * y, ones_ref[...])
            yn = y * lax.rsqrt(ms + EPS)
            o_ref[b, :, ls] = (_silu(g_ref[b, :, ls].astype(F32)) * yn).astype(BF16)


def _retention(rq, rk, rv, rg, consts, batch, seq):
    dec, wc, ws, gam, ones = consts
    t, rw = rq.shape
    c = RET_CHUNK
    pairs = rw // LANES
    blk = pl.BlockSpec((batch, c, rw), lambda i: (0, i, 0))
    whole = lambda a: pl.BlockSpec(a.shape, lambda i: (0,) * a.ndim)
    by_batch = lambda a: a.reshape(batch, seq, rw)
    out = pl.pallas_call(
        _ret_kernel,
        grid=(seq // c,),
        in_specs=[blk, blk, blk, blk, whole(dec), whole(wc), whole(ws), whole(gam), whole(ones)],
        out_specs=blk,
        out_shape=jax.ShapeDtypeStruct((batch, seq, rw), BF16),
        scratch_shapes=[pltpu.VMEM((batch * pairs, LANES, LANES), F32)],
        compiler_params=pltpu.CompilerParams(dimension_semantics=("arbitrary",)),
        name="retention",
    )(by_batch(rq), by_batch(rk), by_batch(rv), by_batch(rg), dec, wc, ws, gam, ones)
    return out.reshape(t, rw)


def _conv_kernel(a_ref, gt_ref, w_ref, b_ref, lg_ref, lb_ref, ones_ref, o_ref, u_ref, sh_ref):
    s = pl.program_id(1)
    ts = a_ref.shape[0]
    sub = 64

    @pl.when(s == 0)
    def _():
        u_ref[0:CONV_HALO, :] = jnp.zeros((CONV_HALO, u_ref.shape[1]), F32)

    u_ref[CONV_HALO:CONV_HALO + ts, :] = a_ref[...] * jax.nn.sigmoid(gt_ref[...])
    span = sh_ref.shape[1]
    for res in range(1, SUBLANES):
        sh_ref[res - 1] = u_ref[res:res + span, :]
    first_tap = CONV_HALO - (CONV_KERNEL - 1)

    def tap_rows(kk, r):
        res, base = (first_tap + kk) % SUBLANES, (first_tap + kk) // SUBLANES * SUBLANES
        if res == 0:
            return u_ref[r + base:r + base + sub, :]
        return sh_ref[res - 1, r + base:r + base + sub, :]

    for r in range(0, ts, sub):
        acc = jnp.broadcast_to(b_ref[...], (sub, u_ref.shape[1]))
        for kk in range(CONV_KERNEL):
            acc = acc + w_ref[kk:kk + 1, :] * tap_rows(kk, r)
        mu = _group_mean(acc, ones_ref[...])
        dlt = acc - mu
        var = _group_mean(dlt * dlt, ones_ref[...])
        yn = dlt * lax.rsqrt(var + EPS) * lg_ref[...] + lb_ref[...]
        o_ref[r:r + sub, :] = _silu(yn).astype(BF16)
    u_ref[0:CONV_HALO, :] = u_ref[ts:ts + CONV_HALO, :]


def _conv(ca, cg, w, b, lg, lb, ones, batch, seq):
    t, cw = ca.shape
    ts = 512
    ns = seq // ts
    blk = pl.BlockSpec((ts, cw), lambda bb, i: (bb * ns + i, 0))
    const = lambda shape: pl.BlockSpec(shape, lambda bb, i: (0, 0))
    return pl.pallas_call(
        _conv_kernel,
        grid=(batch, ns),
        in_specs=[blk, blk, const(w.shape), const((1, cw)), const((1, cw)), const((1, cw)), const((cw, cw))],
        out_specs=blk,
        out_shape=jax.ShapeDtypeStruct((t, cw), BF16),
        scratch_shapes=[pltpu.VMEM((CONV_HALO + ts, cw), F32),
                        pltpu.VMEM((SUBLANES - 1, ts + CONV_HALO - SUBLANES, cw), F32)],
        compiler_params=pltpu.CompilerParams(dimension_semantics=("parallel", "arbitrary")),
        name="conv",
    )(ca, cg, w, b, lg, lb, ones)


MOBA_ONES_ROWS = 16
MOBA_GROUP = 4
MOBA_PREP_LEAD = 8


def _moba_kernel(q_ref, k_ref, v_ref, km_ref, o_ref, vt_ref, qa_ref, acc_ref, s0_ref, s1_ref, p0_ref, p1_ref,
                 al_ref):
    i = pl.program_id(2)
    bs = MOBA_BLOCK
    nblk = k_ref.shape[0] // bs
    unit = nblk // MOBA_GROUP
    rows = 2 * bs
    hd = HEAD_DIM
    tiles = (4 * unit - 1 - i, 2 * unit + i, 2 * unit - 1 - i, i)
    static_past = (3 * unit, 2 * unit, unit, 0)

    @pl.when(i == 0)
    def _():
        for h in range(2):
            vt_ref[h, hd:hd + MOBA_ONES_ROWS, :] = jnp.ones((MOBA_ONES_ROWS, vt_ref.shape[2]), BF16)

        def tr(n, carry):
            st = pl.multiple_of(n * bs, bs)
            v_t = v_ref[pl.ds(st, bs), :].astype(F32).T.astype(BF16)
            vt_ref[0, 0:hd, pl.ds(st, bs)] = v_t[0:hd]
            vt_ref[1, 0:hd, pl.ds(st, bs)] = v_t[hd:2 * hd]
            return carry
        lax.fori_loop(0, nblk, tr, 0)

    s_refs = (s0_ref, s1_ref)
    p_refs = (p0_ref, p1_ref)
    neg_inf = jnp.float32(-jnp.inf)

    def pv_heads(st, p_t):
        return [_dot(vt_ref[h, :, pl.ds(st, bs)], p_t[:, h * bs:(h + 1) * bs]) for h in range(2)]

    def prep(w):
        tile = tiles[w]
        own = pl.multiple_of(tile * bs, bs)
        q_t = q_ref[pl.ds(own, bs), :].T
        q_t = jnp.concatenate([q_t, q_t], axis=1)
        dim_first = lax.broadcasted_iota(jnp.int32, q_t.shape, 0) < hd
        col_first = lax.broadcasted_iota(jnp.int32, q_t.shape, 1) < bs
        qs_t = jnp.where(dim_first == col_first, q_t, 0.0)

        qh, ql = _split_bf16(qs_t)
        kh, kl = _split_bf16(km_ref[0])
        gate = _dot(kh, qh) + _dot(kh, ql) + _dot(kl, qh)
        nidx = lax.broadcasted_iota(jnp.int32, gate.shape, 0).astype(F32)
        gate = jnp.where(nidx < tile.astype(F32), gate, neg_inf)
        bias = jnp.full(gate.shape, NEG_BIG, F32)
        for _ in range(MOBA_TOPK):
            mx = jnp.max(gate, axis=0, keepdims=True)
            pick = jnp.min(jnp.where(gate == mx, nidx, float(LANES)), axis=0, keepdims=True)
            hit = nidx == pick
            bias = jnp.where(hit & (mx > neg_inf), 0.0, bias)
            gate = jnp.where(hit, neg_inf, gate)
        qaug_t = jnp.concatenate([qs_t.astype(BF16), bias.astype(BF16),
                                  jnp.full((LANES - nblk, rows), NEG_BIG, BF16)], axis=0)
        qa_ref[w] = qaug_t

    n_first = unit - 1 - i
    seq = []
    for w in range(MOBA_GROUP):
        seq.append(("own", w))
        seq += [("static", w, s) for s in range(static_past[w])]
    for wa, wb in ((0, 1), (2, 3)):
        seq += [("dyn", wa, wb, d) for d in range(unit - 1)]
    first_use = {w: seq.index(("own", w)) for w in range(MOBA_GROUP)}
    sel_row = lax.broadcasted_iota(jnp.int32, (16, LANES), 1)

    def step_of(step):
        kind = step[0]
        if kind == "own":
            w = step[1]
            return kind, None, w, None, pl.multiple_of(tiles[w] * bs, bs)
        if kind == "static":
            _, w, s = step
            return kind, None, w, s, s * bs
        _, wa, wb, d = step
        is_b = d >= n_first
        blk = jnp.where(is_b, static_past[wb] + d - n_first, static_past[wa] + d)
        return kind, is_b, (wa, wb), blk, pl.multiple_of(blk * bs, bs)

    def slot_index(kind, is_b, w):
        return jnp.where(is_b, w[1], w[0]) if kind == "dyn" else w

    def stage_a(step, slot):
        kind, is_b, w, blk, st = step_of(step)
        if kind == "own":
            sel = jnp.zeros((bs, LANES), BF16)
        else:
            sel = jnp.tile(jnp.where(sel_row == blk, 1.0, 0.0).astype(BF16), (bs // 16, 1))
        sc = _dot(jnp.concatenate([k_ref[pl.ds(st, bs), :], sel], axis=1), qa_ref[slot_index(kind, is_b, w)])
        if kind == "own":
            kpos = lax.broadcasted_iota(jnp.int32, sc.shape, 0)
            qpos = lax.broadcasted_iota(jnp.int32, sc.shape, 1) % bs
            sc = jnp.where(kpos <= qpos, sc, NEG_BIG)
        s_refs[slot][...] = sc

    def stage_b(step, slot, m):
        kind, is_b, w, _, _ = step_of(step)
        sc = s_refs[slot][...]
        m_loc = jnp.max(sc, axis=0, keepdims=True)
        m = list(m)
        if kind == "own":
            m_new = m_loc
            m[w] = m_new
        elif kind == "static":
            m_new = jnp.maximum(m[w], m_loc)
            al_ref[slot:slot + 1, :] = jnp.exp2(m[w] - m_new)
            m[w] = m_new
        else:
            wa, wb = w
            m_old = jnp.where(is_b, m[wb], m[wa])
            m_new = jnp.maximum(m_old, m_loc)
            al_ref[slot:slot + 1, :] = jnp.exp2(m_old - m_new)
            m[wa], m[wb] = jnp.where(is_b, m[wa], m_new), jnp.where(is_b, m_new, m[wb])
        p_refs[slot][...] = jnp.exp2(sc - m_new).astype(BF16)
        return m

    def stage_c(step, slot):
        kind, is_b, w, _, st = step_of(step)
        w = slot_index(kind, is_b, w)
        pv = pv_heads(st, p_refs[slot][...])
        if kind == "own":
            for h in range(2):
                acc_ref[w, h] = pv[h]
        else:
            al = al_ref[slot:slot + 1, :]
            for h in range(2):
                acc_ref[w, h] = acc_ref[w, h] * al[:, h * bs:(h + 1) * bs] + pv[h]

    prep(0)
    m = [None] * MOBA_GROUP
    steps = len(seq)
    for t in range(steps + 2):
        for w in range(1, MOBA_GROUP):
            if t == max(first_use[w] - MOBA_PREP_LEAD, 0):
                prep(w)
        if t < steps:
            stage_a(seq[t], t % 2)
        if 1 <= t <= steps:
            m = stage_b(seq[t - 1], (t - 1) % 2, m)
        if t >= 2:
            stage_c(seq[t - 2], t % 2)

    for w in range(MOBA_GROUP):
        outs = []
        for h in range(2):
            acc = acc_ref[w, h]
            outs.append(acc[0:hd, :] * (1.0 / acc[hd:hd + 1, :]))
        own = pl.multiple_of(tiles[w] * bs, bs)
        o_ref[pl.ds(own, bs), :] = jnp.concatenate(outs, axis=0).T.astype(BF16)


def _moba(mq, mk, mv, kmean, batch, seq):
    t, mw = mq.shape
    bs = MOBA_BLOCK
    nq = seq // bs
    assert nq % MOBA_GROUP == 0
    pairs = mw // LANES
    seqblk = pl.BlockSpec((seq, LANES), lambda b, p, i: (b, p))
    return pl.pallas_call(
        _moba_kernel,
        grid=(batch, pairs, nq // MOBA_GROUP),
        in_specs=[seqblk, seqblk, seqblk, pl.BlockSpec((1, nq, LANES), lambda b, p, i: (b, 0, p))],
        out_specs=seqblk,
        out_shape=jax.ShapeDtypeStruct((t, mw), BF16),
        scratch_shapes=[pltpu.VMEM((2, HEAD_DIM + MOBA_ONES_ROWS, seq), BF16),
                        pltpu.VMEM((MOBA_GROUP, 2 * LANES, 2 * bs), BF16),
                        pltpu.VMEM((MOBA_GROUP, 2, HEAD_DIM + MOBA_ONES_ROWS, bs), F32),
                        pltpu.VMEM((bs, 2 * bs), F32), pltpu.VMEM((bs, 2 * bs), F32),
                        pltpu.VMEM((bs, 2 * bs), BF16), pltpu.VMEM((bs, 2 * bs), BF16),
                        pltpu.VMEM((8, 2 * bs), F32)],
        compiler_params=pltpu.CompilerParams(dimension_semantics=("parallel", "parallel", "arbitrary"),
                                             vmem_limit_bytes=VMEM_LIMIT),
        name="moba",
    )(mq, mk, mv, kmean)


def _lane_constants():
    d = np.arange(LANES) % HEAD_DIM
    half_ret = HEAD_DIM // 2
    ret_inv = RET_ROT_BASE ** (-jnp.linspace(0.0, 1.0, half_ret, dtype=F32))
    half_rope = ROPE_DIMS // 2
    rope_inv = ROPE_THETA ** (-jnp.arange(half_rope, dtype=F32) / half_rope)
    inv_r = ret_inv[d % half_ret]
    sgn_r = jnp.asarray(np.where(d < half_ret, -1.0, 1.0), F32)
    inv_m = jnp.where(jnp.asarray(d < ROPE_DIMS), rope_inv[d % half_rope], 0.0)
    sgn_m = jnp.asarray(np.where(d < half_rope, -1.0, 1.0), F32)
    zero = jnp.zeros((LANES,), F32)
    return jnp.stack([inv_r, sgn_r, inv_m, sgn_m, zero, zero, zero, zero])


def _retention_constants(heads):
    c = RET_CHUNK
    pairs = heads // 2
    log_gamma = jnp.log1p(-jnp.exp2(-5.0 - jnp.arange(heads, dtype=F32)))
    lg = log_gamma[:, None]
    idx = jnp.arange(c, dtype=F32)
    rel = idx[:, None] - idx[None, :]
    decay_in = jnp.where(rel >= 0, jnp.exp(lg[:, :, None] * jnp.maximum(rel, 0.0)), 0.0)
    w_state = jnp.exp(lg * (c - 1 - idx))
    w_cross = jnp.exp(lg * (idx + 1.0))
    chunk_decay = jnp.exp(log_gamma * c)

    def lanes(w):
        return jnp.repeat(w.reshape(pairs, 2, c).transpose(0, 2, 1), HEAD_DIM, axis=2)

    gam = jnp.repeat(chunk_decay.reshape(pairs, 1, 2), HEAD_DIM, axis=2)
    gam = jnp.broadcast_to(gam, (pairs, LANES, LANES))
    return decay_in, lanes(w_cross), lanes(w_state), gam


def _group_ones(width):
    g = np.arange(width) // HEAD_DIM
    return jnp.asarray((g[:, None] == g[None, :]) / HEAD_DIM, BF16)


def kernel(x, positions, ffn1_norm, ffn1_wg, ffn1_wu, ffn1_wd, mix_norm, w_in, conv_w, conv_b, conv_ln_g,
           conv_ln_b, w_out, ffn2_norm, ffn2_wg, ffn2_wu, ffn2_wd, final_norm):
    batch, seq, d = x.shape
    depth = w_in.shape[0]
    cw = conv_w.shape[2]
    rw = mw = (w_out.shape[1] - cw) // 2
    assert seq % MOBA_BLOCK == 0 and seq % RET_CHUNK == 0
    assert w_in.shape[2] == 4 * rw + 3 * mw + 2 * cw
    t = batch * seq

    tabs = _rotary_tables(positions.reshape(t, 1), _lane_constants())
    ret_consts = _retention_constants(rw // HEAD_DIM) + (_group_ones(LANES),)
    conv_ones = _group_ones(cw)

    xf = x.reshape(t, d)
    row = lambda v: v.reshape(1, -1)
    gains = lambda v: v.reshape(depth, 1, d)
    ffn1 = (gains(ffn1_norm), ffn1_wg.astype(BF16), ffn1_wu.astype(BF16), ffn1_wd.astype(BF16))
    ffn2 = (gains(ffn2_norm), ffn2_wg.astype(BF16), ffn2_wu.astype(BF16), ffn2_wd.astype(BF16))
    w_in_b, w_out_b, mix_g = w_in.astype(BF16), w_out.astype(BF16), gains(mix_norm)
    for l in range(depth):
        xf = _ffn(xf, l, *ffn1)
        rq, rk, rv, rg, mq, mk, mv, ca, cg, km = _inproj(xf, l, mix_g, w_in_b, tabs, rw, mw, cw)
        y_ret = _retention(rq, rk, rv, rg, ret_consts, batch, seq)
        y_moba = _moba(mq, mk, mv, km.reshape(batch, seq // MOBA_BLOCK, mw), batch, seq)
        y_conv = _conv(ca, cg, conv_w[l], row(conv_b[l]), row(conv_ln_g[l]), row(conv_ln_b[l]), conv_ones,
                       batch, seq)
        xf = _ffn(xf, l, *ffn2, mix=(y_ret, y_moba, y_conv, w_out_b),
                  final_g=row(final_norm) if l == depth - 1 else None)
    return xf.reshape(batch, seq, d)
```

```python
import functools

import numpy as np
import jax
import jax.numpy as jnp
from jax import lax
from jax.experimental import pallas as pl
from jax.experimental.pallas import tpu as pltpu

HEAD_DIM = 64
RET_CHUNK = 256
RET_ROT_BASE = 10000.0
MOBA_BLOCK = 256
MOBA_TOPK = 3
ROPE_THETA = 500000.0
ROPE_DIMS = HEAD_DIM // 4
CONV_KERNEL = 31
FFN_RES_WEIGHT = 0.5
EPS = 1e-6

LANES = 128
SUBLANES = 8
CONV_HALO = 32
NEG_BIG = -1e30
LOG2_E = 1.4426950408889634
VMEM_LIMIT = 56 * 1024 * 1024

F32 = jnp.float32
BF16 = jnp.bfloat16


def _dot(a, b):
    return jnp.dot(a, b, preferred_element_type=F32)


def _dot_nt(a, b):
    return lax.dot_general(a, b, (((1,), (1,)), ((), ())), preferred_element_type=F32)


def _dot_tn(a, b):
    return lax.dot_general(a, b, (((0,), (0,)), ((), ())), preferred_element_type=F32)


def _split_bf16(x):
    hi = x.astype(BF16)
    lo = (x - hi.astype(F32)).astype(BF16)
    return hi, lo


def _group_mean(x, ones_bf16):
    return _dot(x.astype(BF16), ones_bf16)


def _rms(x, g):
    return x * lax.rsqrt(jnp.mean(x * x, axis=-1, keepdims=True) + EPS) * g


def _silu(x):
    return x * jax.nn.sigmoid(x)


def _table_kernel(pos_ref, inv_ref, cr_ref, sr_ref, cm_ref, sm_ref):
    tm = pos_ref.shape[0]
    half = tm // 2
    lo = lax.broadcasted_iota(jnp.int32, (half, LANES), 1) < HEAD_DIM
    pos = jnp.where(lo, pos_ref[0:half, :].astype(F32), pos_ref[half:tm, :].astype(F32))

    def spread(v, ref):
        r = pltpu.roll(v, HEAD_DIM, 1)
        ref[0:half, :] = jnp.where(lo, v, r)
        ref[half:tm, :] = jnp.where(lo, r, v)

    ang_r = pos * inv_ref[0:1, :]
    ang_m = pos * inv_ref[2:3, :]
    spread(jnp.cos(ang_r), cr_ref)
    spread(jnp.sin(ang_r) * inv_ref[1:2, :], sr_ref)
    spread(jnp.cos(ang_m), cm_ref)
    spread(jnp.sin(ang_m) * inv_ref[3:4, :], sm_ref)


def _rotary_tables(pos_col, inv):
    t = pos_col.shape[0]
    tm = 1024
    tab = jax.ShapeDtypeStruct((t, LANES), F32)
    spec = pl.BlockSpec((tm, LANES), lambda i: (i, 0))
    return pl.pallas_call(
        _table_kernel,
        grid=(t // tm,),
        in_specs=[pl.BlockSpec((tm, 1), lambda i: (i, 0)), pl.BlockSpec((8, LANES), lambda i: (0, 0))],
        out_specs=[spec] * 4,
        out_shape=[tab] * 4,
        name="rotary_tables",
    )(pos_col, inv)


def _ffn_kernel(*refs, with_mix, with_final):
    refs = list(refs)
    x_ref = refs.pop(0)
    if with_mix:
        yr_ref, ym_ref, yc_ref, wo_ref = refs[:4]
        refs = refs[4:]
    g_ref, wg_ref, wu_ref, wd_ref = refs[:4]
    refs = refs[4:]
    if with_final:
        fg_ref = refs.pop(0)
    (o_ref,) = refs

    for r in range(0, x_ref.shape[0], FFN_SUB_ROWS):
        rs = slice(r, r + FFN_SUB_ROWS)
        x = x_ref[rs, :]
        if with_mix:
            mixed = jnp.concatenate([yr_ref[rs, :], ym_ref[rs, :], yc_ref[rs, :]], axis=1)
            x = x + _dot(mixed, wo_ref[...])
        h = _rms(x, g_ref[...]).astype(BF16)
        a = (_silu(_dot(h, wg_ref[...])) * _dot(h, wu_ref[...])).astype(BF16)
        y = x + FFN_RES_WEIGHT * _dot(a, wd_ref[...])
        if with_final:
            y = _rms(y, fg_ref[...])
        o_ref[rs, :] = y


FFN_SUB_ROWS = 256


def _layer_resident(w, layer):
    return pl.BlockSpec((None,) + w.shape[1:], lambda i: (layer,) + (0,) * (w.ndim - 1),
                        pipeline_mode=pl.Buffered(1))


def _ffn(x, layer, g, wg, wu, wd, mix=None, final_g=None):
    t, d = x.shape
    tm = 2 * FFN_SUB_ROWS
    with_mix = mix is not None
    with_final = final_g is not None
    row = lambda i: (i, 0)
    resident = lambda w: _layer_resident(w, layer)
    args = [x]
    specs = [pl.BlockSpec((tm, d), row)]
    if with_mix:
        yr, ym, yc, wo = mix
        args += [yr, ym, yc, wo]
        specs += [pl.BlockSpec((tm, yr.shape[1]), row), pl.BlockSpec((tm, ym.shape[1]), row),
                  pl.BlockSpec((tm, yc.shape[1]), row), resident(wo)]
    args += [g, wg, wu, wd]
    specs += [resident(g), resident(wg), resident(wu), resident(wd)]
    if with_final:
        args.append(final_g)
        specs.append(pl.BlockSpec(final_g.shape, lambda i: (0, 0)))
    return pl.pallas_call(
        functools.partial(_ffn_kernel, with_mix=with_mix, with_final=with_final),
        grid=(t // tm,),
        in_specs=specs,
        out_specs=pl.BlockSpec((tm, d), row),
        out_shape=jax.ShapeDtypeStruct((t, d), F32),
        compiler_params=pltpu.CompilerParams(dimension_semantics=("parallel",), vmem_limit_bytes=VMEM_LIMIT),
        name="ffn",
    )(*args)


def _swap_halves(z, half):
    n = z.shape[1]
    d = lax.broadcasted_iota(jnp.int32, z.shape, 1) % HEAD_DIM
    return jnp.where(d < half, pltpu.roll(z, n - half, 1), pltpu.roll(z, half, 1))


def _inproj_kernel(x_ref, g_ref, w_ref, cr_ref, sr_ref, cm_ref, sm_ref,
                   rq_ref, rk_ref, rv_ref, rg_ref, mq_ref, mk_ref, mvt_ref, ca_ref, cg_ref, km_ref,
                   *, rw, mw, cw):
    scale = HEAD_DIM ** -0.5

    def rot(z, c, s, half):
        reps = z.shape[1] // LANES
        c = jnp.concatenate([c] * reps, axis=1)
        s = jnp.concatenate([s] * reps, axis=1)
        return z * c + _swap_halves(z, half) * s

    for j in range(x_ref.shape[0] // MOBA_BLOCK):
        rs = slice(j * MOBA_BLOCK, (j + 1) * MOBA_BLOCK)
        z = _dot(_rms(x_ref[rs, :], g_ref[...]).astype(BF16), w_ref[...])
        cr, sr = cr_ref[rs, :], sr_ref[rs, :]
        cm, sm = cm_ref[rs, :], sm_ref[rs, :]
        off = 0
        rq_ref[rs, :] = rot(z[:, off:off + rw], cr, sr, HEAD_DIM // 2).astype(BF16)
        off += rw
        rk_ref[rs, :] = (rot(z[:, off:off + rw], cr, sr, HEAD_DIM // 2) * scale).astype(BF16)
        off += rw
        rv_ref[rs, :] = z[:, off:off + rw].astype(BF16)
        off += rw
        rg_ref[rs, :] = z[:, off:off + rw].astype(BF16)
        off += rw
        mq_ref[rs, :] = rot(z[:, off:off + mw], cm, sm, ROPE_DIMS // 2) * (scale * LOG2_E)
        off += mw
        mk = rot(z[:, off:off + mw], cm, sm, ROPE_DIMS // 2)
        mk_ref[rs, :] = mk.astype(BF16)
        km_ref[0, j:j + 1, :] = jnp.mean(mk, axis=0, keepdims=True)
        off += mw
        mvt_ref[:, rs] = z[:, off:off + mw].T.astype(BF16)
        off += mw
        ca_ref[rs, :] = z[:, off:off + cw]
        off += cw
        cg_ref[rs, :] = z[:, off:off + cw]


def _inproj(x, layer, g, w_in, tabs, rw, mw, cw, batch, seq):
    t, d = x.shape
    tm = 512
    assert seq % tm == 0
    per_seq = seq // tm
    row = lambda i: (i, 0)
    nb = tm // MOBA_BLOCK
    out_shape = ([jax.ShapeDtypeStruct((t, rw), BF16)] * 4
                 + [jax.ShapeDtypeStruct((t, mw), F32), jax.ShapeDtypeStruct((t, mw), BF16),
                    jax.ShapeDtypeStruct((batch, mw, seq), BF16)]
                 + [jax.ShapeDtypeStruct((t, cw), F32)] * 2
                 + [jax.ShapeDtypeStruct((t // tm, nb, mw), F32)])
    out_specs = ([pl.BlockSpec((tm, rw), row)] * 4 + [pl.BlockSpec((tm, mw), row)] * 2
                 + [pl.BlockSpec((None, mw, tm), lambda i: (i // per_seq, 0, i % per_seq))]
                 + [pl.BlockSpec((tm, cw), row)] * 2 + [pl.BlockSpec((1, nb, mw), lambda i: (i, 0, 0))])
    tab_spec = pl.BlockSpec((tm, LANES), row)
    return pl.pallas_call(
        functools.partial(_inproj_kernel, rw=rw, mw=mw, cw=cw),
        grid=(t // tm,),
        in_specs=[pl.BlockSpec((tm, d), row), _layer_resident(g, layer), _layer_resident(w_in, layer)]
        + [tab_spec] * 4,
        out_specs=out_specs,
        out_shape=out_shape,
        compiler_params=pltpu.CompilerParams(dimension_semantics=("parallel",), vmem_limit_bytes=VMEM_LIMIT),
        name="inproj",
    )(x, g, w_in, *tabs)


def _ret_kernel(q_ref, k_ref, v_ref, g_ref, dec_ref, wc_ref, ws_ref, gam_ref, ones_ref, o_ref, st_ref):
    batch = q_ref.shape[0]
    pairs = q_ref.shape[2] // LANES

    @pl.when(pl.program_id(0) == 0)
    def _():
        st_ref[...] = jnp.zeros_like(st_ref)

    for b in range(batch):
        for p in range(pairs):
            ls = slice(p * LANES, (p + 1) * LANES)
            q, k, v = q_ref[b, :, ls], k_ref[b, :, ls], v_ref[b, :, ls]
            lane = lax.broadcasted_iota(jnp.int32, q.shape, 1)
            first = lane < HEAD_DIM
            zero = jnp.zeros_like(q)
            p0 = (_dot_nt(jnp.where(first, q, zero), k) * dec_ref[2 * p]).astype(BF16)
            p1 = (_dot_nt(jnp.where(first, zero, q), k) * dec_ref[2 * p + 1]).astype(BF16)
            inner = jnp.where(first, _dot(p0, v), _dot(p1, v))

            ch = b * pairs + p
            st = st_ref[ch]
            qc = (q.astype(F32) * wc_ref[p]).astype(BF16)
            y = inner + _dot(qc, st.astype(BF16))

            kw = (k.astype(F32) * ws_ref[p]).astype(BF16)
            r = lax.broadcasted_iota(jnp.int32, st.shape, 0) < HEAD_DIM
            cc = lax.broadcasted_iota(jnp.int32, st.shape, 1) < HEAD_DIM
            st_ref[ch] = st * gam_ref[p] + jnp.where(r == cc, _dot_tn(kw, v), 0.0)

            ms = _group_mean(y * y, ones_ref[...])
            yn = y * lax.rsqrt(ms + EPS)
            o_ref[b, :, ls] = (_silu(g_ref[b, :, ls].astype(F32)) * yn).astype(BF16)


def _retention(rq, rk, rv, rg, consts, batch, seq):
    dec, wc, ws, gam, ones = consts
    t, rw = rq.shape
    c = RET_CHUNK
    pairs = rw // LANES
    blk = pl.BlockSpec((batch, c, rw), lambda i: (0, i, 0))
    whole = lambda a: pl.BlockSpec(a.shape, lambda i: (0,) * a.ndim)
    by_batch = lambda a: a.reshape(batch, seq, rw)
    out = pl.pallas_call(
        _ret_kernel,
        grid=(seq // c,),
        in_specs=[blk, blk, blk, blk, whole(dec), whole(wc), whole(ws), whole(gam), whole(ones)],
        out_specs=blk,
        out_shape=jax.ShapeDtypeStruct((batch, seq, rw), BF16),
        scratch_shapes=[pltpu.VMEM((batch * pairs, LANES, LANES), F32)],
        compiler_params=pltpu.CompilerParams(dimension_semantics=("arbitrary",)),
        name="retention",
    )(by_batch(rq), by_batch(rk), by_batch(rv), by_batch(rg), dec, wc, ws, gam, ones)
    return out.reshape(t, rw)


def _conv_kernel(a_ref, gt_ref, w_ref, b_ref, lg_ref, lb_ref, ones_ref, o_ref, u_ref, sh_ref):
    s = pl.program_id(1)
    ts = a_ref.shape[0]
    sub = 64

    @pl.when(s == 0)
    def _():
        u_ref[0:CONV_HALO, :] = jnp.zeros((CONV_HALO, u_ref.shape[1]), F32)

    u_ref[CONV_HALO:CONV_HALO + ts, :] = a_ref[...] * jax.nn.sigmoid(gt_ref[...])
    span = sh_ref.shape[1]
    for res in range(1, SUBLANES):
        sh_ref[res - 1] = u_ref[res:res + span, :]
    first_tap = CONV_HALO - (CONV_KERNEL - 1)

    def tap_rows(kk, r):
        res, base = (first_tap + kk) % SUBLANES, (first_tap + kk) // SUBLANES * SUBLANES
        if res == 0:
            return u_ref[r + base:r + base + sub, :]
        return sh_ref[res - 1, r + base:r + base + sub, :]

    for r in range(0, ts, sub):
        acc = jnp.broadcast_to(b_ref[...], (sub, u_ref.shape[1]))
        for kk in range(CONV_KERNEL):
            acc = acc + w_ref[kk:kk + 1, :] * tap_rows(kk, r)
        mu = _group_mean(acc, ones_ref[...])
        dlt = acc - mu
        var = _group_mean(dlt * dlt, ones_ref[...])
        yn = dlt * lax.rsqrt(var + EPS) * lg_ref[...] + lb_ref[...]
        o_ref[r:r + sub, :] = _silu(yn).astype(BF16)
    u_ref[0:CONV_HALO, :] = u_ref[ts:ts + CONV_HALO, :]


def _conv(ca, cg, w, b, lg, lb, ones, batch, seq):
    t, cw = ca.shape
    ts = 512
    ns = seq // ts
    blk = pl.BlockSpec((ts, cw), lambda bb, i: (bb * ns + i, 0))
    const = lambda shape: pl.BlockSpec(shape, lambda bb, i: (0, 0))
    return pl.pallas_call(
        _conv_kernel,
        grid=(batch, ns),
        in_specs=[blk, blk, const(w.shape), const((1, cw)), const((1, cw)), const((1, cw)), const((cw, cw))],
        out_specs=blk,
        out_shape=jax.ShapeDtypeStruct((t, cw), BF16),
        scratch_shapes=[pltpu.VMEM((CONV_HALO + ts, cw), F32),
                        pltpu.VMEM((SUBLANES - 1, ts + CONV_HALO - SUBLANES, cw), F32)],
        compiler_params=pltpu.CompilerParams(dimension_semantics=("parallel", "arbitrary")),
        name="conv",
    )(ca, cg, w, b, lg, lb, ones)


MOBA_ONES_ROWS = 16


def _moba_kernel(q_ref, k_ref, vt_ref, km_ref, o_ref, qa_ref, acc_ref, s0_ref, s1_ref, p0_ref, p1_ref, al_ref):
    i = pl.program_id(2)
    bs = MOBA_BLOCK
    nblk = k_ref.shape[0] // bs
    rows = 2 * bs
    hd = HEAD_DIM
    tiles = (nblk - 1 - i, i)

    s_refs = (s0_ref, s1_ref)
    p_refs = (p0_ref, p1_ref)
    neg_inf = jnp.float32(-jnp.inf)
    ones_rows = jnp.ones((MOBA_ONES_ROWS, bs), BF16)

    def pv_heads(st, p_t):
        return [_dot(jnp.concatenate([vt_ref[h * hd:(h + 1) * hd, pl.ds(st, bs)], ones_rows], axis=0),
                     p_t[:, h * bs:(h + 1) * bs]) for h in range(2)]

    def prep(w):
        tile = tiles[w]
        own = pl.multiple_of(tile * bs, bs)
        q_t = q_ref[pl.ds(own, bs), :].T
        q_t = jnp.concatenate([q_t, q_t], axis=1)
        dim_first = lax.broadcasted_iota(jnp.int32, q_t.shape, 0) < hd
        col_first = lax.broadcasted_iota(jnp.int32, q_t.shape, 1) < bs
        qs_t = jnp.where(dim_first == col_first, q_t, 0.0)

        qh, ql = _split_bf16(qs_t)
        kh, kl = _split_bf16(km_ref[0])
        gate = _dot(kh, qh) + _dot(kh, ql) + _dot(kl, qh)
        nidx = lax.broadcasted_iota(jnp.int32, gate.shape, 0).astype(F32)
        gate = jnp.where(nidx < tile.astype(F32), gate, neg_inf)
        bias = jnp.full(gate.shape, NEG_BIG, F32)
        for _ in range(MOBA_TOPK):
            mx = jnp.max(gate, axis=0, keepdims=True)
            pick = jnp.min(jnp.where(gate == mx, nidx, float(LANES)), axis=0, keepdims=True)
            hit = nidx == pick
            bias = jnp.where(hit & (mx > neg_inf), 0.0, bias)
            gate = jnp.where(hit, neg_inf, gate)
        qaug_t = jnp.concatenate([qs_t.astype(BF16), bias.astype(BF16),
                                  jnp.full((LANES - nblk, rows), NEG_BIG, BF16)], axis=0)
        qa_ref[w] = qaug_t

    n_long = tiles[0]
    half = nblk // 2
    seq = [("own", 0)] + [("past", s) for s in range(half)] + [("own", 1)] + [("past", s) for s in range(half, nblk - 1)]
    sel_row = lax.broadcasted_iota(jnp.int32, (16, LANES), 1)

    def step_of(step):
        kind, v = step
        if kind == "own":
            return True, None, v, None, pl.multiple_of(tiles[v] * bs, bs)
        if v < half:
            return False, False, 0, v, v * bs
        is_b = v >= n_long
        blk = jnp.where(is_b, v - n_long, v)
        return False, is_b, is_b.astype(jnp.int32), blk, pl.multiple_of(blk * bs, bs)

    def stage_a(step, slot):
        own, _, w, blk, st = step_of(step)
        if own:
            sel = jnp.zeros((bs, LANES), BF16)
        else:
            sel = jnp.tile(jnp.where(sel_row == blk, 1.0, 0.0).astype(BF16), (bs // 16, 1))
        sc = _dot(jnp.concatenate([k_ref[pl.ds(st, bs), :], sel], axis=1), qa_ref[w])
        if own:
            kpos = lax.broadcasted_iota(jnp.int32, sc.shape, 0)
            qpos = lax.broadcasted_iota(jnp.int32, sc.shape, 1) % bs
            sc = jnp.where(kpos <= qpos, sc, NEG_BIG)
        s_refs[slot][...] = sc

    def stage_b(step, slot, m_pair):
        own, is_b, w, _, _ = step_of(step)
        sc = s_refs[slot][...]
        m_loc = jnp.max(sc, axis=0, keepdims=True)
        if own:
            m_new = m_loc
            m_pair = tuple(m_new if j == w else m_pair[j] for j in range(2))
        else:
            if is_b is False:
                m_old = m_pair[0]
            else:
                m_old = jnp.where(is_b, m_pair[1], m_pair[0])
            m_new = jnp.maximum(m_old, m_loc)
            al_ref[slot:slot + 1, :] = jnp.exp2(m_old - m_new)
            if is_b is False:
                m_pair = (m_new, m_pair[1])
            else:
                m_pair = (jnp.where(is_b, m_pair[0], m_new), jnp.where(is_b, m_new, m_pair[1]))
        p_refs[slot][...] = jnp.exp2(sc - m_new).astype(BF16)
        return m_pair

    def stage_c(step, slot):
        own, _, w, _, st = step_of(step)
        pv = pv_heads(st, p_refs[slot][...])
        if own:
            for h in range(2):
                acc_ref[w, h] = pv[h]
        else:
            al = al_ref[slot:slot + 1, :]
            for h in range(2):
                acc_ref[w, h] = acc_ref[w, h] * al[:, h * bs:(h + 1) * bs] + pv[h]

    prep(0)
    prep(1)
    m_pair = (None, None)
    steps = len(seq)
    for t in range(steps + 2):
        if t < steps:
            stage_a(seq[t], t % 2)
        if 1 <= t <= steps:
            m_pair = stage_b(seq[t - 1], (t - 1) % 2, m_pair)
        if t >= 2:
            stage_c(seq[t - 2], t % 2)

    for w in range(2):
        outs = []
        for h in range(2):
            acc = acc_ref[w, h]
            outs.append(acc[0:hd, :] * (1.0 / acc[hd:hd + 1, :]))
        own = pl.multiple_of(tiles[w] * bs, bs)
        o_ref[pl.ds(own, bs), :] = jnp.concatenate(outs, axis=0).T.astype(BF16)


def _moba(mq, mk, mvt, kmean, batch, seq):
    t, mw = mq.shape
    bs = MOBA_BLOCK
    nq = seq // bs
    assert nq % 2 == 0
    pairs = mw // LANES
    seqblk = pl.BlockSpec((seq, LANES), lambda b, p, i: (b, p))
    return pl.pallas_call(
        _moba_kernel,
        grid=(batch, pairs, nq // 2),
        in_specs=[seqblk, seqblk, pl.BlockSpec((None, LANES, seq), lambda b, p, i: (b, p, 0)),
                  pl.BlockSpec((1, nq, LANES), lambda b, p, i: (b, 0, p))],
        out_specs=seqblk,
        out_shape=jax.ShapeDtypeStruct((t, mw), BF16),
        scratch_shapes=[pltpu.VMEM((2, 2 * LANES, 2 * bs), BF16),
                        pltpu.VMEM((2, 2, HEAD_DIM + MOBA_ONES_ROWS, bs), F32),
                        pltpu.VMEM((bs, 2 * bs), F32), pltpu.VMEM((bs, 2 * bs), F32),
                        pltpu.VMEM((bs, 2 * bs), BF16), pltpu.VMEM((bs, 2 * bs), BF16),
                        pltpu.VMEM((8, 2 * bs), F32)],
        compiler_params=pltpu.CompilerParams(dimension_semantics=("parallel", "parallel", "arbitrary"),
                                             vmem_limit_bytes=VMEM_LIMIT),
        name="moba",
    )(mq, mk, mvt, kmean)


def _lane_constants():
    d = np.arange(LANES) % HEAD_DIM
    half_ret = HEAD_DIM // 2
    ret_inv = RET_ROT_BASE ** (-jnp.linspace(0.0, 1.0, half_ret, dtype=F32))
    half_rope = ROPE_DIMS // 2
    rope_inv = ROPE_THETA ** (-jnp.arange(half_rope, dtype=F32) / half_rope)
    inv_r = ret_inv[d % half_ret]
    sgn_r = jnp.asarray(np.where(d < half_ret, -1.0, 1.0), F32)
    inv_m = jnp.where(jnp.asarray(d < ROPE_DIMS), rope_inv[d % half_rope], 0.0)
    sgn_m = jnp.asarray(np.where(d < half_rope, -1.0, 1.0), F32)
    zero = jnp.zeros((LANES,), F32)
    return jnp.stack([inv_r, sgn_r, inv_m, sgn_m, zero, zero, zero, zero])


def _retention_constants(heads):
    c = RET_CHUNK
    pairs = heads // 2
    log_gamma = jnp.log1p(-jnp.exp2(-5.0 - jnp.arange(heads, dtype=F32)))
    lg = log_gamma[:, None]
    idx = jnp.arange(c, dtype=F32)
    rel = idx[:, None] - idx[None, :]
    decay_in = jnp.where(rel >= 0, jnp.exp(lg[:, :, None] * jnp.maximum(rel, 0.0)), 0.0)
    w_state = jnp.exp(lg * (c - 1 - idx))
    w_cross = jnp.exp(lg * (idx + 1.0))
    chunk_decay = jnp.exp(log_gamma * c)

    def lanes(w):
        return jnp.repeat(w.reshape(pairs, 2, c).transpose(0, 2, 1), HEAD_DIM, axis=2)

    gam = jnp.repeat(chunk_decay.reshape(pairs, 1, 2), HEAD_DIM, axis=2)
    gam = jnp.broadcast_to(gam, (pairs, LANES, LANES))
    return decay_in, lanes(w_cross), lanes(w_state), gam


def _group_ones(width):
    g = np.arange(width) // HEAD_DIM
    return jnp.asarray((g[:, None] == g[None, :]) / HEAD_DIM, BF16)


def kernel(x, positions, ffn1_norm, ffn1_wg, ffn1_wu, ffn1_wd, mix_norm, w_in, conv_w, conv_b, conv_ln_g,
           conv_ln_b, w_out, ffn2_norm, ffn2_wg, ffn2_wu, ffn2_wd, final_norm):
    batch, seq, d = x.shape
    depth = w_in.shape[0]
    cw = conv_w.shape[2]
    rw = mw = (w_out.shape[1] - cw) // 2
    assert seq % MOBA_BLOCK == 0 and seq % RET_CHUNK == 0
    assert w_in.shape[2] == 4 * rw + 3 * mw + 2 * cw
    t = batch * seq

    tabs = _rotary_tables(positions.reshape(t, 1), _lane_constants())
    ret_consts = _retention_constants(rw // HEAD_DIM) + (_group_ones(LANES),)
    conv_ones = _group_ones(cw)

    xf = x.reshape(t, d)
    row = lambda v: v.reshape(1, -1)
    gains = lambda v: v.reshape(depth, 1, d)
    ffn1 = (gains(ffn1_norm), ffn1_wg.astype(BF16), ffn1_wu.astype(BF16), ffn1_wd.astype(BF16))
    ffn2 = (gains(ffn2_norm), ffn2_wg.astype(BF16), ffn2_wu.astype(BF16), ffn2_wd.astype(BF16))
    w_in_b, w_out_b, mix_g = w_in.astype(BF16), w_out.astype(BF16), gains(mix_norm)
    for l in range(depth):
        xf = _ffn(xf, l, *ffn1)
        rq, rk, rv, rg, mq, mk, mvt, ca, cg, km = _inproj(xf, l, mix_g, w_in_b, tabs, rw, mw, cw, batch, seq)
        y_ret = _retention(rq, rk, rv, rg, ret_consts, batch, seq)
        y_moba = _moba(mq, mk, mvt, km.reshape(batch, seq // MOBA_BLOCK, mw), batch, seq)
        y_conv = _conv(ca, cg, conv_w[l], row(conv_b[l]), row(conv_ln_g[l]), row(conv_ln_b[l]), conv_ones,
                       batch, seq)
        xf = _ffn(xf, l, *ffn2, mix=(y_ret, y_moba, y_conv, w_out_b),
                  final_g=row(final_norm) if l == depth - 1 else None)
    return xf.reshape(batch, seq, d)
```

```python
import functools

import numpy as np
import jax
import jax.numpy as jnp
from jax import lax
from jax.experimental import pallas as pl
from jax.experimental.pallas import tpu as pltpu

HEAD_DIM = 64
RET_CHUNK = 256
RET_ROT_BASE = 10000.0
MOBA_BLOCK = 256
MOBA_TOPK = 3
ROPE_THETA = 500000.0
ROPE_DIMS = HEAD_DIM // 4
CONV_KERNEL = 31
FFN_RES_WEIGHT = 0.5
EPS = 1e-6

LANES = 128
SUBLANES = 8
CONV_HALO = 32
NEG_BIG = -1e30
LOG2_E = 1.4426950408889634
VMEM_LIMIT = 56 * 1024 * 1024

F32 = jnp.float32
BF16 = jnp.bfloat16


def _dot(a, b):
    return jnp.dot(a, b, preferred_element_type=F32)


def _dot_nt(a, b):
    return lax.dot_general(a, b, (((1,), (1,)), ((), ())), preferred_element_type=F32)


def _dot_tn(a, b):
    return lax.dot_general(a, b, (((0,), (0,)), ((), ())), preferred_element_type=F32)


def _split_bf16(x):
    hi = x.astype(BF16)
    lo = (x - hi.astype(F32)).astype(BF16)
    return hi, lo


def _group_mean(x, ones_bf16):
    return _dot(x.astype(BF16), ones_bf16)


def _rms(x, g):
    return x * lax.rsqrt(jnp.mean(x * x, axis=-1, keepdims=True) + EPS) * g


def _silu(x):
    return x * jax.nn.sigmoid(x)


def _table_kernel(pos_ref, inv_ref, cr_ref, sr_ref, cm_ref, sm_ref):
    tm = pos_ref.shape[0]
    half = tm // 2
    lo = lax.broadcasted_iota(jnp.int32, (half, LANES), 1) < HEAD_DIM
    pos = jnp.where(lo, pos_ref[0:half, :].astype(F32), pos_ref[half:tm, :].astype(F32))

    def spread(v, ref):
        r = pltpu.roll(v, HEAD_DIM, 1)
        ref[0:half, :] = jnp.where(lo, v, r)
        ref[half:tm, :] = jnp.where(lo, r, v)

    ang_r = pos * inv_ref[0:1, :]
    ang_m = pos * inv_ref[2:3, :]
    spread(jnp.cos(ang_r), cr_ref)
    spread(jnp.sin(ang_r) * inv_ref[1:2, :], sr_ref)
    spread(jnp.cos(ang_m), cm_ref)
    spread(jnp.sin(ang_m) * inv_ref[3:4, :], sm_ref)


def _rotary_tables(pos_col, inv):
    t = pos_col.shape[0]
    tm = 1024
    tab = jax.ShapeDtypeStruct((t, LANES), F32)
    spec = pl.BlockSpec((tm, LANES), lambda i: (i, 0))
    return pl.pallas_call(
        _table_kernel,
        grid=(t // tm,),
        in_specs=[pl.BlockSpec((tm, 1), lambda i: (i, 0)), pl.BlockSpec((8, LANES), lambda i: (0, 0))],
        out_specs=[spec] * 4,
        out_shape=[tab] * 4,
        name="rotary_tables",
    )(pos_col, inv)


def _ffn_kernel(*refs, with_mix, with_final):
    refs = list(refs)
    x_ref = refs.pop(0)
    if with_mix:
        yr_ref, ym_ref, yc_ref, wo_ref = refs[:4]
        refs = refs[4:]
    g_ref, wg_ref, wu_ref, wd_ref = refs[:4]
    refs = refs[4:]
    if with_final:
        fg_ref = refs.pop(0)
    (o_ref,) = refs

    for r in range(0, x_ref.shape[0], FFN_SUB_ROWS):
        rs = slice(r, r + FFN_SUB_ROWS)
        x = x_ref[rs, :]
        if with_mix:
            mixed = jnp.concatenate([yr_ref[rs, :], ym_ref[rs, :], yc_ref[rs, :]], axis=1)
            x = x + _dot(mixed, wo_ref[...])
        h = _rms(x, g_ref[...]).astype(BF16)
        a = (_silu(_dot(h, wg_ref[...])) * _dot(h, wu_ref[...])).astype(BF16)
        y = x + FFN_RES_WEIGHT * _dot(a, wd_ref[...])
        if with_final:
            y = _rms(y, fg_ref[...])
        o_ref[rs, :] = y


FFN_SUB_ROWS = 256


def _layer_resident(w, layer):
    return pl.BlockSpec((None,) + w.shape[1:], lambda i: (layer,) + (0,) * (w.ndim - 1),
                        pipeline_mode=pl.Buffered(1))


def _ffn(x, layer, g, wg, wu, wd, mix=None, final_g=None):
    t, d = x.shape
    tm = 2 * FFN_SUB_ROWS
    with_mix = mix is not None
    with_final = final_g is not None
    row = lambda i: (i, 0)
    resident = lambda w: _layer_resident(w, layer)
    args = [x]
    specs = [pl.BlockSpec((tm, d), row)]
    if with_mix:
        yr, ym, yc, wo = mix
        args += [yr, ym, yc, wo]
        specs += [pl.BlockSpec((tm, yr.shape[1]), row), pl.BlockSpec((tm, ym.shape[1]), row),
                  pl.BlockSpec((tm, yc.shape[1]), row), resident(wo)]
    args += [g, wg, wu, wd]
    specs += [resident(g), resident(wg), resident(wu), resident(wd)]
    if with_final:
        args.append(final_g)
        specs.append(pl.BlockSpec(final_g.shape, lambda i: (0, 0)))
    return pl.pallas_call(
        functools.partial(_ffn_kernel, with_mix=with_mix, with_final=with_final),
        grid=(t // tm,),
        in_specs=specs,
        out_specs=pl.BlockSpec((tm, d), row),
        out_shape=jax.ShapeDtypeStruct((t, d), F32),
        compiler_params=pltpu.CompilerParams(dimension_semantics=("parallel",), vmem_limit_bytes=VMEM_LIMIT),
        name="ffn",
    )(*args)


def _swap_halves(z, half):
    n = z.shape[1]
    d = lax.broadcasted_iota(jnp.int32, z.shape, 1) % HEAD_DIM
    return jnp.where(d < half, pltpu.roll(z, n - half, 1), pltpu.roll(z, half, 1))


def _inproj_kernel(x_ref, g_ref, w_ref, cr_ref, sr_ref, cm_ref, sm_ref,
                   rq_ref, rk_ref, rv_ref, rg_ref, mq_ref, mk_ref, mvt_ref, ca_ref, cg_ref, km_ref,
                   *, rw, mw, cw):
    scale = HEAD_DIM ** -0.5

    def rot(z, c, s, half):
        reps = z.shape[1] // LANES
        c = jnp.concatenate([c] * reps, axis=1)
        s = jnp.concatenate([s] * reps, axis=1)
        return z * c + _swap_halves(z, half) * s

    for j in range(x_ref.shape[0] // MOBA_BLOCK):
        rs = slice(j * MOBA_BLOCK, (j + 1) * MOBA_BLOCK)
        z = _dot(_rms(x_ref[rs, :], g_ref[...]).astype(BF16), w_ref[...])
        cr, sr = cr_ref[rs, :], sr_ref[rs, :]
        cm, sm = cm_ref[rs, :], sm_ref[rs, :]
        off = 0
        rq_ref[rs, :] = rot(z[:, off:off + rw], cr, sr, HEAD_DIM // 2).astype(BF16)
        off += rw
        rk_ref[rs, :] = (rot(z[:, off:off + rw], cr, sr, HEAD_DIM // 2) * scale).astype(BF16)
        off += rw
        rv_ref[rs, :] = z[:, off:off + rw].astype(BF16)
        off += rw
        rg_ref[rs, :] = z[:, off:off + rw].astype(BF16)
        off += rw
        mq_ref[rs, :] = rot(z[:, off:off + mw], cm, sm, ROPE_DIMS // 2) * (scale * LOG2_E)
        off += mw
        mk = rot(z[:, off:off + mw], cm, sm, ROPE_DIMS // 2)
        mk_ref[rs, :] = mk.astype(BF16)
        km_ref[0, j:j + 1, :] = jnp.mean(mk, axis=0, keepdims=True)
        off += mw
        mvt_ref[:, rs] = z[:, off:off + mw].T.astype(BF16)
        off += mw
        ca_ref[rs, :] = z[:, off:off + cw]
        off += cw
        cg_ref[rs, :] = z[:, off:off + cw]


def _inproj(x, layer, g, w_in, tabs, rw, mw, cw, batch, seq):
    t, d = x.shape
    tm = 512
    assert seq % tm == 0
    per_seq = seq // tm
    row = lambda i: (i, 0)
    nb = tm // MOBA_BLOCK
    out_shape = ([jax.ShapeDtypeStruct((t, rw), BF16)] * 4
                 + [jax.ShapeDtypeStruct((t, mw), F32), jax.ShapeDtypeStruct((t, mw), BF16),
                    jax.ShapeDtypeStruct((batch, mw, seq), BF16)]
                 + [jax.ShapeDtypeStruct((t, cw), F32)] * 2
                 + [jax.ShapeDtypeStruct((t // tm, nb, mw), F32)])
    out_specs = ([pl.BlockSpec((tm, rw), row)] * 4 + [pl.BlockSpec((tm, mw), row)] * 2
                 + [pl.BlockSpec((None, mw, tm), lambda i: (i // per_seq, 0, i % per_seq))]
                 + [pl.BlockSpec((tm, cw), row)] * 2 + [pl.BlockSpec((1, nb, mw), lambda i: (i, 0, 0))])
    tab_spec = pl.BlockSpec((tm, LANES), row)
    return pl.pallas_call(
        functools.partial(_inproj_kernel, rw=rw, mw=mw, cw=cw),
        grid=(t // tm,),
        in_specs=[pl.BlockSpec((tm, d), row), _layer_resident(g, layer), _layer_resident(w_in, layer)]
        + [tab_spec] * 4,
        out_specs=out_specs,
        out_shape=out_shape,
        compiler_params=pltpu.CompilerParams(dimension_semantics=("parallel",), vmem_limit_bytes=VMEM_LIMIT),
        name="inproj",
    )(x, g, w_in, *tabs)


CONV_SUB_ROWS = 64


def _retconv_kernel(q_ref, k_ref, v_ref, g_ref, a_ref, gt_ref, dec_ref, wc_ref, ws_ref, gam_ref, ones_ref,
                    cw_ref, cb_ref, lg_ref, lb_ref, cones_ref, o_ref, oc_ref, st_ref, u_ref, sh_ref):
    batch = q_ref.shape[0]
    pairs = q_ref.shape[2] // LANES
    ts = a_ref.shape[1]
    first_tap = CONV_HALO - (CONV_KERNEL - 1)

    @pl.when(pl.program_id(0) == 0)
    def _():
        st_ref[...] = jnp.zeros_like(st_ref)
        u_ref[:, 0:CONV_HALO, :] = jnp.zeros((batch, CONV_HALO, u_ref.shape[2]), F32)

    span = sh_ref.shape[2]
    for b in range(batch):
        u_ref[b, CONV_HALO:CONV_HALO + ts, :] = a_ref[b] * jax.nn.sigmoid(gt_ref[b])
        for res in range(1, SUBLANES):
            sh_ref[b, res - 1] = u_ref[b, res:res + span, :]

    def conv_rows(b, r):
        n = CONV_SUB_ROWS
        acc = jnp.broadcast_to(cb_ref[...], (n, u_ref.shape[2]))
        for kk in range(CONV_KERNEL):
            res, base = (first_tap + kk) % SUBLANES, (first_tap + kk) // SUBLANES * SUBLANES
            rows = (u_ref[b, r + base:r + base + n, :] if res == 0
                    else sh_ref[b, res - 1, r + base:r + base + n, :])
            acc = acc + cw_ref[kk:kk + 1, :] * rows
        mu = _group_mean(acc, cones_ref[...])
        dlt = acc - mu
        var = _group_mean(dlt * dlt, cones_ref[...])
        yn = dlt * lax.rsqrt(var + EPS) * lg_ref[...] + lb_ref[...]
        oc_ref[b, r:r + n, :] = _silu(yn).astype(BF16)

    def chain(b, p):
        ls = slice(p * LANES, (p + 1) * LANES)
        q, k, v = q_ref[b, :, ls], k_ref[b, :, ls], v_ref[b, :, ls]
        lane = lax.broadcasted_iota(jnp.int32, q.shape, 1)
        first = lane < HEAD_DIM
        zero = jnp.zeros_like(q)
        p0 = (_dot_nt(jnp.where(first, q, zero), k) * dec_ref[2 * p]).astype(BF16)
        p1 = (_dot_nt(jnp.where(first, zero, q), k) * dec_ref[2 * p + 1]).astype(BF16)
        inner = jnp.where(first, _dot(p0, v), _dot(p1, v))

        ch = b * pairs + p
        st = st_ref[ch]
        qc = (q.astype(F32) * wc_ref[p]).astype(BF16)
        y = inner + _dot(qc, st.astype(BF16))

        kw = (k.astype(F32) * ws_ref[p]).astype(BF16)
        r = lax.broadcasted_iota(jnp.int32, st.shape, 0) < HEAD_DIM
        cc = lax.broadcasted_iota(jnp.int32, st.shape, 1) < HEAD_DIM
        st_ref[ch] = st * gam_ref[p] + jnp.where(r == cc, _dot_tn(kw, v), 0.0)

        ms = _group_mean(y * y, ones_ref[...])
        yn = y * lax.rsqrt(ms + EPS)
        o_ref[b, :, ls] = (_silu(g_ref[b, :, ls].astype(F32)) * yn).astype(BF16)

    conv_jobs = [(b, r) for b in range(batch) for r in range(0, ts, CONV_SUB_ROWS)]
    chains = [(b, p) for b in range(batch) for p in range(pairs)]
    per_chain = -(-len(conv_jobs) // len(chains))
    for job in chains:
        chain(*job)
        for _ in range(min(per_chain, len(conv_jobs))):
            conv_rows(*conv_jobs.pop(0))
    for job in conv_jobs:
        conv_rows(*job)
    for b in range(batch):
        u_ref[b, 0:CONV_HALO, :] = u_ref[b, ts:ts + CONV_HALO, :]


def _retention_conv(rq, rk, rv, rg, ca, cg, ret_consts, conv_params, batch, seq):
    dec, wc, ws, gam, ones = ret_consts
    t, rw = rq.shape
    cw = ca.shape[1]
    c = RET_CHUNK
    pairs = rw // LANES
    blk = lambda width: pl.BlockSpec((batch, c, width), lambda i: (0, i, 0))
    whole = lambda a: pl.BlockSpec(a.shape, lambda i: (0,) * a.ndim)
    by_batch = lambda a: a.reshape(batch, seq, a.shape[1])
    y_ret, y_conv = pl.pallas_call(
        _retconv_kernel,
        grid=(seq // c,),
        in_specs=[blk(rw)] * 4 + [blk(cw)] * 2 + [whole(a) for a in (dec, wc, ws, gam, ones) + conv_params],
        out_specs=[blk(rw), blk(cw)],
        out_shape=[jax.ShapeDtypeStruct((batch, seq, rw), BF16), jax.ShapeDtypeStruct((batch, seq, cw), BF16)],
        scratch_shapes=[pltpu.VMEM((batch * pairs, LANES, LANES), F32),
                        pltpu.VMEM((batch, CONV_HALO + c, cw), F32),
                        pltpu.VMEM((batch, SUBLANES - 1, c + CONV_HALO - SUBLANES, cw), F32)],
        compiler_params=pltpu.CompilerParams(dimension_semantics=("arbitrary",)),
        name="retention_conv",
    )(*[by_batch(a) for a in (rq, rk, rv, rg, ca, cg)], dec, wc, ws, gam, ones, *conv_params)
    return y_ret.reshape(t, rw), y_conv.reshape(t, cw)


MOBA_ONES_ROWS = 16


def _moba_kernel(q_ref, k_ref, vt_ref, km_ref, o_ref, qa_ref, acc_ref, s0_ref, s1_ref, p0_ref, p1_ref, al_ref):
    i = pl.program_id(2)
    bs = MOBA_BLOCK
    nblk = k_ref.shape[0] // bs
    rows = 2 * bs
    hd = HEAD_DIM
    tiles = (nblk - 1 - i, i)

    s_refs = (s0_ref, s1_ref)
    p_refs = (p0_ref, p1_ref)
    neg_inf = jnp.float32(-jnp.inf)
    ones_rows = jnp.ones((MOBA_ONES_ROWS, bs), BF16)

    def pv_heads(st, p_t):
        return [_dot(jnp.concatenate([vt_ref[h * hd:(h + 1) * hd, pl.ds(st, bs)], ones_rows], axis=0),
                     p_t[:, h * bs:(h + 1) * bs]) for h in range(2)]

    def prep(tile):
        own = pl.multiple_of(tile * bs, bs)
        q_t = q_ref[pl.ds(own, bs), :].T
        q_t = jnp.concatenate([q_t, q_t], axis=1)
        dim_first = lax.broadcasted_iota(jnp.int32, q_t.shape, 0) < hd
        col_first = lax.broadcasted_iota(jnp.int32, q_t.shape, 1) < bs
        qs_t = jnp.where(dim_first == col_first, q_t, 0.0)

        qh, ql = _split_bf16(qs_t)
        kh, kl = _split_bf16(km_ref[0])
        gate = _dot(kh, qh) + _dot(kh, ql) + _dot(kl, qh)
        nidx = lax.broadcasted_iota(jnp.int32, gate.shape, 0).astype(F32)
        gate = jnp.where(nidx < tile.astype(F32), gate, neg_inf)
        bias = jnp.full(gate.shape, NEG_BIG, F32)
        for _ in range(MOBA_TOPK):
            mx = jnp.max(gate, axis=0, keepdims=True)
            pick = jnp.min(jnp.where(gate == mx, nidx, float(LANES)), axis=0, keepdims=True)
            hit = nidx == pick
            bias = jnp.where(hit & (mx > neg_inf), 0.0, bias)
            gate = jnp.where(hit, neg_inf, gate)
        return jnp.concatenate([qs_t.astype(BF16), bias.astype(BF16),
                                jnp.full((LANES - nblk, rows), NEG_BIG, BF16)], axis=0)

    n_long = tiles[0]
    half = nblk // 2
    seq = [("own", 0)] + [("past", s) for s in range(half)] + [("own", 1)] + [("past", s) for s in range(half, nblk - 1)]
    sel_row = lax.broadcasted_iota(jnp.int32, (16, LANES), 1)

    def step_of(step):
        kind, v = step
        if kind == "own":
            return True, None, v, None, pl.multiple_of(tiles[v] * bs, bs)
        if v < half:
            return False, False, 0, v, v * bs
        is_b = v >= n_long
        blk = jnp.where(is_b, v - n_long, v)
        return False, is_b, is_b.astype(jnp.int32), blk, pl.multiple_of(blk * bs, bs)

    def stage_a(step, slot):
        own, _, w, blk, st = step_of(step)
        if own:
            sel = jnp.zeros((bs, LANES), BF16)
        else:
            sel = jnp.tile(jnp.where(sel_row == blk, 1.0, 0.0).astype(BF16), (bs // 16, 1))
        sc = _dot(jnp.concatenate([k_ref[pl.ds(st, bs), :], sel], axis=1), qa_ref[w])
        if own:
            kpos = lax.broadcasted_iota(jnp.int32, sc.shape, 0)
            qpos = lax.broadcasted_iota(jnp.int32, sc.shape, 1) % bs
            sc = jnp.where(kpos <= qpos, sc, NEG_BIG)
        s_refs[slot][...] = sc

    def stage_b(step, slot, m_pair):
        own, is_b, w, _, _ = step_of(step)
        sc = s_refs[slot][...]
        m_loc = jnp.max(sc, axis=0, keepdims=True)
        if own:
            m_new = m_loc
            m_pair = tuple(m_new if j == w else m_pair[j] for j in range(2))
        else:
            if is_b is False:
                m_old = m_pair[0]
            else:
                m_old = jnp.where(is_b, m_pair[1], m_pair[0])
            m_new = jnp.maximum(m_old, m_loc)
            al_ref[slot:slot + 1, :] = jnp.exp2(m_old - m_new)
            if is_b is False:
                m_pair = (m_new, m_pair[1])
            else:
                m_pair = (jnp.where(is_b, m_pair[0], m_new), jnp.where(is_b, m_new, m_pair[1]))
        p_refs[slot][...] = jnp.exp2(sc - m_new).astype(BF16)
        return m_pair

    def stage_c(step, slot):
        own, _, w, _, st = step_of(step)
        pv = pv_heads(st, p_refs[slot][...])
        if own:
            for h in range(2):
                acc_ref[w, h] = pv[h]
        else:
            al = al_ref[slot:slot + 1, :]
            for h in range(2):
                acc_ref[w, h] = acc_ref[w, h] * al[:, h * bs:(h + 1) * bs] + pv[h]

    for w in range(2):
        qa_ref[w] = prep(tiles[w])
    m_pair = (None, None)
    steps = len(seq)
    for t in range(steps + 2):
        if t < steps:
            stage_a(seq[t], t % 2)
        if 1 <= t <= steps:
            m_pair = stage_b(seq[t - 1], (t - 1) % 2, m_pair)
        if t >= 2:
            stage_c(seq[t - 2], t % 2)

    for w in range(2):
        outs = []
        for h in range(2):
            acc = acc_ref[w, h]
            outs.append(acc[0:hd, :] * (1.0 / acc[hd:hd + 1, :]))
        own = pl.multiple_of(tiles[w] * bs, bs)
        o_ref[pl.ds(own, bs), :] = jnp.concatenate(outs, axis=0).T.astype(BF16)


def _moba(mq, mk, mvt, kmean, batch, seq):
    t, mw = mq.shape
    bs = MOBA_BLOCK
    nq = seq // bs
    assert nq % 2 == 0
    pairs = mw // LANES
    seqblk = pl.BlockSpec((seq, LANES), lambda b, p, i: (b, p))
    return pl.pallas_call(
        _moba_kernel,
        grid=(batch, pairs, nq // 2),
        in_specs=[seqblk, seqblk, pl.BlockSpec((None, LANES, seq), lambda b, p, i: (b, p, 0)),
                  pl.BlockSpec((1, nq, LANES), lambda b, p, i: (b, 0, p))],
        out_specs=seqblk,
        out_shape=jax.ShapeDtypeStruct((t, mw), BF16),
        scratch_shapes=[pltpu.VMEM((2, 2 * LANES, 2 * bs), BF16),
                        pltpu.VMEM((2, 2, HEAD_DIM + MOBA_ONES_ROWS, bs), F32),
                        pltpu.VMEM((bs, 2 * bs), F32), pltpu.VMEM((bs, 2 * bs), F32),
                        pltpu.VMEM((bs, 2 * bs), BF16), pltpu.VMEM((bs, 2 * bs), BF16),
                        pltpu.VMEM((8, 2 * bs), F32)],
        compiler_params=pltpu.CompilerParams(dimension_semantics=("parallel", "parallel", "arbitrary"),
                                             vmem_limit_bytes=VMEM_LIMIT),
        name="moba",
    )(mq, mk, mvt, kmean)


def _lane_constants():
    d = np.arange(LANES) % HEAD_DIM
    half_ret = HEAD_DIM // 2
    ret_inv = RET_ROT_BASE ** (-jnp.linspace(0.0, 1.0, half_ret, dtype=F32))
    half_rope = ROPE_DIMS // 2
    rope_inv = ROPE_THETA ** (-jnp.arange(half_rope, dtype=F32) / half_rope)
    inv_r = ret_inv[d % half_ret]
    sgn_r = jnp.asarray(np.where(d < half_ret, -1.0, 1.0), F32)
    inv_m = jnp.where(jnp.asarray(d < ROPE_DIMS), rope_inv[d % half_rope], 0.0)
    sgn_m = jnp.asarray(np.where(d < half_rope, -1.0, 1.0), F32)
    zero = jnp.zeros((LANES,), F32)
    return jnp.stack([inv_r, sgn_r, inv_m, sgn_m, zero, zero, zero, zero])


def _retention_constants(heads):
    c = RET_CHUNK
    pairs = heads // 2
    log_gamma = jnp.log1p(-jnp.exp2(-5.0 - jnp.arange(heads, dtype=F32)))
    lg = log_gamma[:, None]
    idx = jnp.arange(c, dtype=F32)
    rel = idx[:, None] - idx[None, :]
    decay_in = jnp.where(rel >= 0, jnp.exp(lg[:, :, None] * jnp.maximum(rel, 0.0)), 0.0)
    w_state = jnp.exp(lg * (c - 1 - idx))
    w_cross = jnp.exp(lg * (idx + 1.0))
    chunk_decay = jnp.exp(log_gamma * c)

    def lanes(w):
        return jnp.repeat(w.reshape(pairs, 2, c).transpose(0, 2, 1), HEAD_DIM, axis=2)

    gam = jnp.repeat(chunk_decay.reshape(pairs, 1, 2), HEAD_DIM, axis=2)
    gam = jnp.broadcast_to(gam, (pairs, LANES, LANES))
    return decay_in, lanes(w_cross), lanes(w_state), gam


def _group_ones(width):
    g = np.arange(width) // HEAD_DIM
    return jnp.asarray((g[:, None] == g[None, :]) / HEAD_DIM, BF16)


def kernel(x, positions, ffn1_norm, ffn1_wg, ffn1_wu, ffn1_wd, mix_norm, w_in, conv_w, conv_b, conv_ln_g,
           conv_ln_b, w_out, ffn2_norm, ffn2_wg, ffn2_wu, ffn2_wd, final_norm):
    batch, seq, d = x.shape
    depth = w_in.shape[0]
    cw = conv_w.shape[2]
    rw = mw = (w_out.shape[1] - cw) // 2
    assert seq % MOBA_BLOCK == 0 and seq % RET_CHUNK == 0
    assert w_in.shape[2] == 4 * rw + 3 * mw + 2 * cw
    t = batch * seq

    tabs = _rotary_tables(positions.reshape(t, 1), _lane_constants())
    ret_consts = _retention_constants(rw // HEAD_DIM) + (_group_ones(LANES),)
    conv_ones = _group_ones(cw)

    xf = x.reshape(t, d)
    row = lambda v: v.reshape(1, -1)
    gains = lambda v: v.reshape(depth, 1, d)
    ffn1 = (gains(ffn1_norm), ffn1_wg.astype(BF16), ffn1_wu.astype(BF16), ffn1_wd.astype(BF16))
    ffn2 = (gains(ffn2_norm), ffn2_wg.astype(BF16), ffn2_wu.astype(BF16), ffn2_wd.astype(BF16))
    w_in_b, w_out_b, mix_g = w_in.astype(BF16), w_out.astype(BF16), gains(mix_norm)
    for l in range(depth):
        xf = _ffn(xf, l, *ffn1)
        rq, rk, rv, rg, mq, mk, mvt, ca, cg, km = _inproj(xf, l, mix_g, w_in_b, tabs, rw, mw, cw, batch, seq)
        y_ret, y_conv = _retention_conv(
            rq, rk, rv, rg, ca, cg, ret_consts,
            (conv_w[l], row(conv_b[l]), row(conv_ln_g[l]), row(conv_ln_b[l]), conv_ones), batch, seq)
        y_moba = _moba(mq, mk, mvt, km.reshape(batch, seq // MOBA_BLOCK, mw), batch, seq)
        xf = _ffn(xf, l, *ffn2, mix=(y_ret, y_moba, y_conv, w_out_b),
                  final_g=row(final_norm) if l == depth - 1 else None)
    return xf.reshape(batch, seq, d)
```

```python
import functools

import numpy as np
import jax
import jax.numpy as jnp
from jax import lax
from jax.experimental import pallas as pl
from jax.experimental.pallas import tpu as pltpu

HEAD_DIM = 64
RET_CHUNK = 256
RET_ROT_BASE = 10000.0
MOBA_BLOCK = 256
MOBA_TOPK = 3
ROPE_THETA = 500000.0
ROPE_DIMS = HEAD_DIM // 4
CONV_KERNEL = 31
FFN_RES_WEIGHT = 0.5
EPS = 1e-6

LANES = 128
SUBLANES = 8
CONV_HALO = 32
NEG_BIG = -1e30
LOG2_E = 1.4426950408889634
VMEM_LIMIT = 56 * 1024 * 1024

F32 = jnp.float32
BF16 = jnp.bfloat16


def _dot(a, b):
    return jnp.dot(a, b, preferred_element_type=F32)


def _dot_nt(a, b):
    return lax.dot_general(a, b, (((1,), (1,)), ((), ())), preferred_element_type=F32)


def _dot_tn(a, b):
    return lax.dot_general(a, b, (((0,), (0,)), ((), ())), preferred_element_type=F32)


def _split_bf16(x):
    hi = x.astype(BF16)
    lo = (x - hi.astype(F32)).astype(BF16)
    return hi, lo


def _group_mean(x, ones_bf16):
    return _dot(x.astype(BF16), ones_bf16)


def _rms(x, g):
    return x * lax.rsqrt(jnp.mean(x * x, axis=-1, keepdims=True) + EPS) * g


def _silu(x):
    return x * jax.nn.sigmoid(x)


def _table_kernel(pos_ref, inv_ref, cr_ref, sr_ref, cm_ref, sm_ref):
    tm = pos_ref.shape[0]
    half = tm // 2
    lo = lax.broadcasted_iota(jnp.int32, (half, LANES), 1) < HEAD_DIM
    pos = jnp.where(lo, pos_ref[0:half, :].astype(F32), pos_ref[half:tm, :].astype(F32))

    def spread(v, ref):
        r = pltpu.roll(v, HEAD_DIM, 1)
        ref[0:half, :] = jnp.where(lo, v, r)
        ref[half:tm, :] = jnp.where(lo, r, v)

    ang_r = pos * inv_ref[0:1, :]
    ang_m = pos * inv_ref[2:3, :]
    spread(jnp.cos(ang_r), cr_ref)
    spread(jnp.sin(ang_r) * inv_ref[1:2, :], sr_ref)
    spread(jnp.cos(ang_m), cm_ref)
    spread(jnp.sin(ang_m) * inv_ref[3:4, :], sm_ref)


def _rotary_tables(pos_col, inv):
    t = pos_col.shape[0]
    tm = 1024
    tab = jax.ShapeDtypeStruct((t, LANES), F32)
    spec = pl.BlockSpec((tm, LANES), lambda i: (i, 0))
    return pl.pallas_call(
        _table_kernel,
        grid=(t // tm,),
        in_specs=[pl.BlockSpec((tm, 1), lambda i: (i, 0)), pl.BlockSpec((8, LANES), lambda i: (0, 0))],
        out_specs=[spec] * 4,
        out_shape=[tab] * 4,
        name="rotary_tables",
    )(pos_col, inv)


def _ffn_kernel(*refs, with_mix, with_final):
    refs = list(refs)
    x_ref = refs.pop(0)
    if with_mix:
        yr_ref, ym_ref, yc_ref, wo_ref = refs[:4]
        refs = refs[4:]
    g_ref, wg_ref, wu_ref, wd_ref = refs[:4]
    refs = refs[4:]
    if with_final:
        fg_ref = refs.pop(0)
    (o_ref,) = refs

    for r in range(0, x_ref.shape[0], FFN_SUB_ROWS):
        rs = slice(r, r + FFN_SUB_ROWS)
        x = x_ref[rs, :]
        if with_mix:
            mixed = jnp.concatenate([yr_ref[rs, :], ym_ref[rs, :], yc_ref[rs, :]], axis=1)
            x = x + _dot(mixed, wo_ref[...])
        h = _rms(x, g_ref[...]).astype(BF16)
        a = (_silu(_dot(h, wg_ref[...])) * _dot(h, wu_ref[...])).astype(BF16)
        y = x + FFN_RES_WEIGHT * _dot(a, wd_ref[...])
        if with_final:
            y = _rms(y, fg_ref[...])
        o_ref[rs, :] = y


FFN_SUB_ROWS = 256
FFN_SUBTILES = 4


def _layer_resident(w, layer):
    return pl.BlockSpec((None,) + w.shape[1:], lambda i: (layer,) + (0,) * (w.ndim - 1),
                        pipeline_mode=pl.Buffered(1))


def _ffn(x, layer, g, wg, wu, wd, mix=None, final_g=None):
    t, d = x.shape
    tm = FFN_SUBTILES * FFN_SUB_ROWS
    with_mix = mix is not None
    with_final = final_g is not None
    row = lambda i: (i, 0)
    resident = lambda w: _layer_resident(w, layer)
    args = [x]
    specs = [pl.BlockSpec((tm, d), row)]
    if with_mix:
        yr, ym, yc, wo = mix
        args += [yr, ym, yc, wo]
        specs += [pl.BlockSpec((tm, yr.shape[1]), row), pl.BlockSpec((tm, ym.shape[1]), row),
                  pl.BlockSpec((tm, yc.shape[1]), row), resident(wo)]
    args += [g, wg, wu, wd]
    specs += [resident(g), resident(wg), resident(wu), resident(wd)]
    if with_final:
        args.append(final_g)
        specs.append(pl.BlockSpec(final_g.shape, lambda i: (0, 0)))
    return pl.pallas_call(
        functools.partial(_ffn_kernel, with_mix=with_mix, with_final=with_final),
        grid=(t // tm,),
        in_specs=specs,
        out_specs=pl.BlockSpec((tm, d), row),
        out_shape=jax.ShapeDtypeStruct((t, d), F32),
        compiler_params=pltpu.CompilerParams(dimension_semantics=("parallel",), vmem_limit_bytes=VMEM_LIMIT),
        name="ffn",
    )(*args)


def _swap_halves(z, half):
    n = z.shape[1]
    d = lax.broadcasted_iota(jnp.int32, z.shape, 1) % HEAD_DIM
    return jnp.where(d < half, pltpu.roll(z, n - half, 1), pltpu.roll(z, half, 1))


def _inproj_kernel(x_ref, g_ref, w_ref, cr_ref, sr_ref, cm_ref, sm_ref,
                   rq_ref, rk_ref, rv_ref, rg_ref, mq_ref, mk_ref, mvt_ref, ca_ref, cg_ref, km_ref,
                   *, rw, mw, cw):
    scale = HEAD_DIM ** -0.5

    def rot(z, c, s, half):
        reps = z.shape[1] // LANES
        c = jnp.concatenate([c] * reps, axis=1)
        s = jnp.concatenate([s] * reps, axis=1)
        return z * c + _swap_halves(z, half) * s

    for j in range(x_ref.shape[0] // MOBA_BLOCK):
        rs = slice(j * MOBA_BLOCK, (j + 1) * MOBA_BLOCK)
        z = _dot(_rms(x_ref[rs, :], g_ref[...]).astype(BF16), w_ref[...])
        cr, sr = cr_ref[rs, :], sr_ref[rs, :]
        cm, sm = cm_ref[rs, :], sm_ref[rs, :]
        off = 0
        rq_ref[rs, :] = rot(z[:, off:off + rw], cr, sr, HEAD_DIM // 2).astype(BF16)
        off += rw
        rk_ref[rs, :] = (rot(z[:, off:off + rw], cr, sr, HEAD_DIM // 2) * scale).astype(BF16)
        off += rw
        rv_ref[rs, :] = z[:, off:off + rw].astype(BF16)
        off += rw
        rg_ref[rs, :] = z[:, off:off + rw].astype(BF16)
        off += rw
        mq_ref[rs, :] = rot(z[:, off:off + mw], cm, sm, ROPE_DIMS // 2) * (scale * LOG2_E)
        off += mw
        mk = rot(z[:, off:off + mw], cm, sm, ROPE_DIMS // 2)
        mk_ref[rs, :] = mk.astype(BF16)
        km_ref[0, j:j + 1, :] = jnp.mean(mk, axis=0, keepdims=True)
        off += mw
        mvt_ref[:, rs] = z[:, off:off + mw].T.astype(BF16)
        off += mw
        ca_ref[rs, :] = z[:, off:off + cw]
        off += cw
        cg_ref[rs, :] = z[:, off:off + cw]


def _inproj(x, layer, g, w_in, tabs, rw, mw, cw, batch, seq):
    t, d = x.shape
    tm = 1024
    assert seq % tm == 0
    per_seq = seq // tm
    row = lambda i: (i, 0)
    nb = tm // MOBA_BLOCK
    out_shape = ([jax.ShapeDtypeStruct((t, rw), BF16)] * 4
                 + [jax.ShapeDtypeStruct((t, mw), F32), jax.ShapeDtypeStruct((t, mw), BF16),
                    jax.ShapeDtypeStruct((batch, mw, seq), BF16)]
                 + [jax.ShapeDtypeStruct((t, cw), F32)] * 2
                 + [jax.ShapeDtypeStruct((t // tm, nb, mw), F32)])
    out_specs = ([pl.BlockSpec((tm, rw), row)] * 4 + [pl.BlockSpec((tm, mw), row)] * 2
                 + [pl.BlockSpec((None, mw, tm), lambda i: (i // per_seq, 0, i % per_seq))]
                 + [pl.BlockSpec((tm, cw), row)] * 2 + [pl.BlockSpec((1, nb, mw), lambda i: (i, 0, 0))])
    tab_spec = pl.BlockSpec((tm, LANES), row)
    return pl.pallas_call(
        functools.partial(_inproj_kernel, rw=rw, mw=mw, cw=cw),
        grid=(t // tm,),
        in_specs=[pl.BlockSpec((tm, d), row), _layer_resident(g, layer), _layer_resident(w_in, layer)]
        + [tab_spec] * 4,
        out_specs=out_specs,
        out_shape=out_shape,
        compiler_params=pltpu.CompilerParams(dimension_semantics=("parallel",), vmem_limit_bytes=VMEM_LIMIT),
        name="inproj",
    )(x, g, w_in, *tabs)


CONV_SUB_ROWS = 64


def _retconv_kernel(q_ref, k_ref, v_ref, g_ref, a_ref, gt_ref, dec_ref, wc_ref, ws_ref, gam_ref, ones_ref,
                    cw_ref, cb_ref, lg_ref, lb_ref, cones_ref, o_ref, oc_ref, st_ref, u_ref, sh_ref):
    batch = q_ref.shape[0]
    pairs = q_ref.shape[2] // LANES
    ts = a_ref.shape[1]
    first_tap = CONV_HALO - (CONV_KERNEL - 1)

    @pl.when(pl.program_id(0) == 0)
    def _():
        st_ref[...] = jnp.zeros_like(st_ref)
        u_ref[:, 0:CONV_HALO, :] = jnp.zeros((batch, CONV_HALO, u_ref.shape[2]), F32)

    span = sh_ref.shape[2]
    for b in range(batch):
        u_ref[b, CONV_HALO:CONV_HALO + ts, :] = a_ref[b] * jax.nn.sigmoid(gt_ref[b])
        for res in range(1, SUBLANES):
            sh_ref[b, res - 1] = u_ref[b, res:res + span, :]

    def conv_rows(b, r):
        n = CONV_SUB_ROWS
        acc = jnp.broadcast_to(cb_ref[...], (n, u_ref.shape[2]))
        for kk in range(CONV_KERNEL):
            res, base = (first_tap + kk) % SUBLANES, (first_tap + kk) // SUBLANES * SUBLANES
            rows = (u_ref[b, r + base:r + base + n, :] if res == 0
                    else sh_ref[b, res - 1, r + base:r + base + n, :])
            acc = acc + cw_ref[kk:kk + 1, :] * rows
        mu = _group_mean(acc, cones_ref[...])
        dlt = acc - mu
        var = _group_mean(dlt * dlt, cones_ref[...])
        yn = dlt * lax.rsqrt(var + EPS) * lg_ref[...] + lb_ref[...]
        oc_ref[b, r:r + n, :] = _silu(yn).astype(BF16)

    def chain(b, p):
        ls = slice(p * LANES, (p + 1) * LANES)
        q, k, v = q_ref[b, :, ls], k_ref[b, :, ls], v_ref[b, :, ls]
        lane = lax.broadcasted_iota(jnp.int32, q.shape, 1)
        first = lane < HEAD_DIM
        zero = jnp.zeros_like(q)
        p0 = (_dot_nt(jnp.where(first, q, zero), k) * dec_ref[2 * p]).astype(BF16)
        p1 = (_dot_nt(jnp.where(first, zero, q), k) * dec_ref[2 * p + 1]).astype(BF16)
        inner = jnp.where(first, _dot(p0, v), _dot(p1, v))

        ch = b * pairs + p
        st = st_ref[ch]
        qc = (q.astype(F32) * wc_ref[p]).astype(BF16)
        y = inner + _dot(qc, st.astype(BF16))

        kw = (k.astype(F32) * ws_ref[p]).astype(BF16)
        r = lax.broadcasted_iota(jnp.int32, st.shape, 0) < HEAD_DIM
        cc = lax.broadcasted_iota(jnp.int32, st.shape, 1) < HEAD_DIM
        st_ref[ch] = st * gam_ref[p] + jnp.where(r == cc, _dot_tn(kw, v), 0.0)

        ms = _group_mean(y * y, ones_ref[...])
        yn = y * lax.rsqrt(ms + EPS)
        o_ref[b, :, ls] = (_silu(g_ref[b, :, ls].astype(F32)) * yn).astype(BF16)

    conv_jobs = [(b, r) for b in range(batch) for r in range(0, ts, CONV_SUB_ROWS)]
    chains = [(b, p) for b in range(batch) for p in range(pairs)]
    per_chain = -(-len(conv_jobs) // len(chains))
    for job in chains:
        chain(*job)
        for _ in range(min(per_chain, len(conv_jobs))):
            conv_rows(*conv_jobs.pop(0))
    for job in conv_jobs:
        conv_rows(*job)
    for b in range(batch):
        u_ref[b, 0:CONV_HALO, :] = u_ref[b, ts:ts + CONV_HALO, :]


def _retention_conv(rq, rk, rv, rg, ca, cg, ret_consts, conv_params, batch, seq):
    dec, wc, ws, gam, ones = ret_consts
    t, rw = rq.shape
    cw = ca.shape[1]
    c = RET_CHUNK
    pairs = rw // LANES
    blk = lambda width: pl.BlockSpec((batch, c, width), lambda i: (0, i, 0))
    whole = lambda a: pl.BlockSpec(a.shape, lambda i: (0,) * a.ndim)
    by_batch = lambda a: a.reshape(batch, seq, a.shape[1])
    y_ret, y_conv = pl.pallas_call(
        _retconv_kernel,
        grid=(seq // c,),
        in_specs=[blk(rw)] * 4 + [blk(cw)] * 2 + [whole(a) for a in (dec, wc, ws, gam, ones) + conv_params],
        out_specs=[blk(rw), blk(cw)],
        out_shape=[jax.ShapeDtypeStruct((batch, seq, rw), BF16), jax.ShapeDtypeStruct((batch, seq, cw), BF16)],
        scratch_shapes=[pltpu.VMEM((batch * pairs, LANES, LANES), F32),
                        pltpu.VMEM((batch, CONV_HALO + c, cw), F32),
                        pltpu.VMEM((batch, SUBLANES - 1, c + CONV_HALO - SUBLANES, cw), F32)],
        compiler_params=pltpu.CompilerParams(dimension_semantics=("arbitrary",)),
        name="retention_conv",
    )(*[by_batch(a) for a in (rq, rk, rv, rg, ca, cg)], dec, wc, ws, gam, ones, *conv_params)
    return y_ret.reshape(t, rw), y_conv.reshape(t, cw)


MOBA_ONES_ROWS = 16


def _moba_kernel(q_ref, k_ref, vt_ref, km_ref, o_ref, qa_ref, acc_ref, s0_ref, s1_ref, p0_ref, p1_ref, al_ref):
    i = pl.program_id(2)
    bs = MOBA_BLOCK
    nblk = k_ref.shape[0] // bs
    rows = 2 * bs
    hd = HEAD_DIM
    tiles = (nblk - 1 - i, i)

    s_refs = (s0_ref, s1_ref)
    p_refs = (p0_ref, p1_ref)
    neg_inf = jnp.float32(-jnp.inf)
    ones_rows = jnp.ones((MOBA_ONES_ROWS, bs), BF16)

    def pv_heads(st, p_t):
        return [_dot(jnp.concatenate([vt_ref[h * hd:(h + 1) * hd, pl.ds(st, bs)], ones_rows], axis=0),
                     p_t[:, h * bs:(h + 1) * bs]) for h in range(2)]

    def prep(tile):
        own = pl.multiple_of(tile * bs, bs)
        q_t = q_ref[pl.ds(own, bs), :].T
        q_t = jnp.concatenate([q_t, q_t], axis=1)
        dim_first = lax.broadcasted_iota(jnp.int32, q_t.shape, 0) < hd
        col_first = lax.broadcasted_iota(jnp.int32, q_t.shape, 1) < bs
        qs_t = jnp.where(dim_first == col_first, q_t, 0.0)

        qh, ql = _split_bf16(qs_t)
        kh, kl = _split_bf16(km_ref[0])
        gate = _dot(kh, qh) + _dot(kh, ql) + _dot(kl, qh)
        nidx = lax.broadcasted_iota(jnp.int32, gate.shape, 0).astype(F32)
        gate = jnp.where(nidx < tile.astype(F32), gate, neg_inf)
        bias = jnp.full(gate.shape, NEG_BIG, F32)
        for _ in range(MOBA_TOPK):
            mx = jnp.max(gate, axis=0, keepdims=True)
            pick = jnp.min(jnp.where(gate == mx, nidx, float(LANES)), axis=0, keepdims=True)
            hit = nidx == pick
            bias = jnp.where(hit & (mx > neg_inf), 0.0, bias)
            gate = jnp.where(hit, neg_inf, gate)
        return jnp.concatenate([qs_t.astype(BF16), bias.astype(BF16),
                                jnp.full((LANES - nblk, rows), NEG_BIG, BF16)], axis=0)

    n_long = tiles[0]
    half = nblk // 2
    seq = [("own", 0)] + [("past", s) for s in range(half)] + [("own", 1)] + [("past", s) for s in range(half, nblk - 1)]
    sel_row = lax.broadcasted_iota(jnp.int32, (16, LANES), 1)

    def step_of(step):
        kind, v = step
        if kind == "own":
            return True, None, v, None, pl.multiple_of(tiles[v] * bs, bs)
        if v < half:
            return False, False, 0, v, v * bs
        is_b = v >= n_long
        blk = jnp.where(is_b, v - n_long, v)
        return False, is_b, is_b.astype(jnp.int32), blk, pl.multiple_of(blk * bs, bs)

    def stage_a(step, slot):
        own, _, w, blk, st = step_of(step)
        if own:
            sel = jnp.zeros((bs, LANES), BF16)
        else:
            sel = jnp.tile(jnp.where(sel_row == blk, 1.0, 0.0).astype(BF16), (bs // 16, 1))
        sc = _dot(jnp.concatenate([k_ref[pl.ds(st, bs), :], sel], axis=1), qa_ref[w])
        if own:
            kpos = lax.broadcasted_iota(jnp.int32, sc.shape, 0)
            qpos = lax.broadcasted_iota(jnp.int32, sc.shape, 1) % bs
            sc = jnp.where(kpos <= qpos, sc, NEG_BIG)
        s_refs[slot][...] = sc

    def stage_b(step, slot, m_pair):
        own, is_b, w, _, _ = step_of(step)
        sc = s_refs[slot][...]
        m_loc = jnp.max(sc, axis=0, keepdims=True)
        if own:
            m_new = m_loc
            m_pair = tuple(m_new if j == w else m_pair[j] for j in range(2))
        else:
            if is_b is False:
                m_old = m_pair[0]
            else:
                m_old = jnp.where(is_b, m_pair[1], m_pair[0])
            m_new = jnp.maximum(m_old, m_loc)
            al_ref[slot:slot + 1, :] = jnp.exp2(m_old - m_new)
            if is_b is False:
                m_pair = (m_new, m_pair[1])
            else:
                m_pair = (jnp.where(is_b, m_pair[0], m_new), jnp.where(is_b, m_new, m_pair[1]))
        p_refs[slot][...] = jnp.exp2(sc - m_new).astype(BF16)
        return m_pair

    def stage_c(step, slot):
        own, _, w, _, st = step_of(step)
        pv = pv_heads(st, p_refs[slot][...])
        if own:
            for h in range(2):
                acc_ref[w, h] = pv[h]
        else:
            al = al_ref[slot:slot + 1, :]
            for h in range(2):
                acc_ref[w, h] = acc_ref[w, h] * al[:, h * bs:(h + 1) * bs] + pv[h]

    for w in range(2):
        qa_ref[w] = prep(tiles[w])
    m_pair = (None, None)
    steps = len(seq)
    for t in range(steps + 2):
        if t < steps:
            stage_a(seq[t], t % 2)
        if 1 <= t <= steps:
            m_pair = stage_b(seq[t - 1], (t - 1) % 2, m_pair)
        if t >= 2:
            stage_c(seq[t - 2], t % 2)

    for w in range(2):
        outs = []
        for h in range(2):
            acc = acc_ref[w, h]
            outs.append(acc[0:hd, :] * (1.0 / acc[hd:hd + 1, :]))
        own = pl.multiple_of(tiles[w] * bs, bs)
        o_ref[pl.ds(own, bs), :] = jnp.concatenate(outs, axis=0).T.astype(BF16)


def _moba(mq, mk, mvt, kmean, batch, seq):
    t, mw = mq.shape
    bs = MOBA_BLOCK
    nq = seq // bs
    assert nq % 2 == 0
    pairs = mw // LANES
    seqblk = pl.BlockSpec((seq, LANES), lambda b, p, i: (b, p))
    return pl.pallas_call(
        _moba_kernel,
        grid=(batch, pairs, nq // 2),
        in_specs=[seqblk, seqblk, pl.BlockSpec((None, LANES, seq), lambda b, p, i: (b, p, 0)),
                  pl.BlockSpec((1, nq, LANES), lambda b, p, i: (b, 0, p))],
        out_specs=seqblk,
        out_shape=jax.ShapeDtypeStruct((t, mw), BF16),
        scratch_shapes=[pltpu.VMEM((2, 2 * LANES, 2 * bs), BF16),
                        pltpu.VMEM((2, 2, HEAD_DIM + MOBA_ONES_ROWS, bs), F32),
                        pltpu.VMEM((bs, 2 * bs), F32), pltpu.VMEM((bs, 2 * bs), F32),
                        pltpu.VMEM((bs, 2 * bs), BF16), pltpu.VMEM((bs, 2 * bs), BF16),
                        pltpu.VMEM((8, 2 * bs), F32)],
        compiler_params=pltpu.CompilerParams(dimension_semantics=("parallel", "parallel", "arbitrary"),
                                             vmem_limit_bytes=VMEM_LIMIT),
        name="moba",
    )(mq, mk, mvt, kmean)


def _lane_constants():
    d = np.arange(LANES) % HEAD_DIM
    half_ret = HEAD_DIM // 2
    ret_inv = RET_ROT_BASE ** (-jnp.linspace(0.0, 1.0, half_ret, dtype=F32))
    half_rope = ROPE_DIMS // 2
    rope_inv = ROPE_THETA ** (-jnp.arange(half_rope, dtype=F32) / half_rope)
    inv_r = ret_inv[d % half_ret]
    sgn_r = jnp.asarray(np.where(d < half_ret, -1.0, 1.0), F32)
    inv_m = jnp.where(jnp.asarray(d < ROPE_DIMS), rope_inv[d % half_rope], 0.0)
    sgn_m = jnp.asarray(np.where(d < half_rope, -1.0, 1.0), F32)
    zero = jnp.zeros((LANES,), F32)
    return jnp.stack([inv_r, sgn_r, inv_m, sgn_m, zero, zero, zero, zero])


def _retention_constants(heads):
    c = RET_CHUNK
    pairs = heads // 2
    log_gamma = jnp.log1p(-jnp.exp2(-5.0 - jnp.arange(heads, dtype=F32)))
    lg = log_gamma[:, None]
    idx = jnp.arange(c, dtype=F32)
    rel = idx[:, None] - idx[None, :]
    decay_in = jnp.where(rel >= 0, jnp.exp(lg[:, :, None] * jnp.maximum(rel, 0.0)), 0.0)
    w_state = jnp.exp(lg * (c - 1 - idx))
    w_cross = jnp.exp(lg * (idx + 1.0))
    chunk_decay = jnp.exp(log_gamma * c)

    def lanes(w):
        return jnp.repeat(w.reshape(pairs, 2, c).transpose(0, 2, 1), HEAD_DIM, axis=2)

    gam = jnp.repeat(chunk_decay.reshape(pairs, 1, 2), HEAD_DIM, axis=2)
    gam = jnp.broadcast_to(gam, (pairs, LANES, LANES))
    return decay_in, lanes(w_cross), lanes(w_state), gam


def _group_ones(width):
    g = np.arange(width) // HEAD_DIM
    return jnp.asarray((g[:, None] == g[None, :]) / HEAD_DIM, BF16)


def kernel(x, positions, ffn1_norm, ffn1_wg, ffn1_wu, ffn1_wd, mix_norm, w_in, conv_w, conv_b, conv_ln_g,
           conv_ln_b, w_out, ffn2_norm, ffn2_wg, ffn2_wu, ffn2_wd, final_norm):
    batch, seq, d = x.shape
    depth = w_in.shape[0]
    cw = conv_w.shape[2]
    rw = mw = (w_out.shape[1] - cw) // 2
    assert seq % MOBA_BLOCK == 0 and seq % RET_CHUNK == 0
    assert w_in.shape[2] == 4 * rw + 3 * mw + 2 * cw
    t = batch * seq

    tabs = _rotary_tables(positions.reshape(t, 1), _lane_constants())
    ret_consts = _retention_constants(rw // HEAD_DIM) + (_group_ones(LANES),)
    conv_ones = _group_ones(cw)

    xf = x.reshape(t, d)
    row = lambda v: v.reshape(1, -1)
    gains = lambda v: v.reshape(depth, 1, d)
    ffn1 = (gains(ffn1_norm), ffn1_wg.astype(BF16), ffn1_wu.astype(BF16), ffn1_wd.astype(BF16))
    ffn2 = (gains(ffn2_norm), ffn2_wg.astype(BF16), ffn2_wu.astype(BF16), ffn2_wd.astype(BF16))
    w_in_b, w_out_b, mix_g = w_in.astype(BF16), w_out.astype(BF16), gains(mix_norm)
    for l in range(depth):
        xf = _ffn(xf, l, *ffn1)
        rq, rk, rv, rg, mq, mk, mvt, ca, cg, km = _inproj(xf, l, mix_g, w_in_b, tabs, rw, mw, cw, batch, seq)
        y_ret, y_conv = _retention_conv(
            rq, rk, rv, rg, ca, cg, ret_consts,
            (conv_w[l], row(conv_b[l]), row(conv_ln_g[l]), row(conv_ln_b[l]), conv_ones), batch, seq)
        y_moba = _moba(mq, mk, mvt, km.reshape(batch, seq // MOBA_BLOCK, mw), batch, seq)
        xf = _ffn(xf, l, *ffn2, mix=(y_ret, y_moba, y_conv, w_out_b),
                  final_g=row(final_norm) if l == depth - 1 else None)
    return xf.reshape(batch, seq, d)
```

```python
import functools

import numpy as np
import jax
import jax.numpy as jnp
from jax import lax
from jax.experimental import pallas as pl
from jax.experimental.pallas import tpu as pltpu

HEAD_DIM = 64
RET_CHUNK = 256
RET_ROT_BASE = 10000.0
MOBA_BLOCK = 256
MOBA_TOPK = 3
ROPE_THETA = 500000.0
ROPE_DIMS = HEAD_DIM // 4
CONV_KERNEL = 31
FFN_RES_WEIGHT = 0.5
EPS = 1e-6

LANES = 128
SUBLANES = 8
CONV_HALO = 32
NEG_BIG = -1e30
LOG2_E = 1.4426950408889634
VMEM_LIMIT = 56 * 1024 * 1024

F32 = jnp.float32
BF16 = jnp.bfloat16


def _dot(a, b):
    return jnp.dot(a, b, preferred_element_type=F32)


def _dot_nt(a, b):
    return lax.dot_general(a, b, (((1,), (1,)), ((), ())), preferred_element_type=F32)


def _dot_tn(a, b):
    return lax.dot_general(a, b, (((0,), (0,)), ((), ())), preferred_element_type=F32)


def _split_bf16(x):
    hi = x.astype(BF16)
    lo = (x - hi.astype(F32)).astype(BF16)
    return hi, lo


def _group_mean(x, ones_bf16):
    return _dot(x.astype(BF16), ones_bf16)


def _rms(x, g):
    return x * lax.rsqrt(jnp.mean(x * x, axis=-1, keepdims=True) + EPS) * g


def _silu(x):
    return x * jax.nn.sigmoid(x)


def _table_kernel(pos_ref, inv_ref, cr_ref, sr_ref, cm_ref, sm_ref):
    tm = pos_ref.shape[0]
    half = tm // 2
    lo = lax.broadcasted_iota(jnp.int32, (half, LANES), 1) < HEAD_DIM
    pos = jnp.where(lo, pos_ref[0:half, :].astype(F32), pos_ref[half:tm, :].astype(F32))

    def spread(v, ref):
        r = pltpu.roll(v, HEAD_DIM, 1)
        ref[0:half, :] = jnp.where(lo, v, r)
        ref[half:tm, :] = jnp.where(lo, r, v)

    ang_r = pos * inv_ref[0:1, :]
    ang_m = pos * inv_ref[2:3, :]
    spread(jnp.cos(ang_r), cr_ref)
    spread(jnp.sin(ang_r) * inv_ref[1:2, :], sr_ref)
    spread(jnp.cos(ang_m), cm_ref)
    spread(jnp.sin(ang_m) * inv_ref[3:4, :], sm_ref)


def _rotary_tables(pos_col, inv):
    t = pos_col.shape[0]
    tm = 1024
    tab = jax.ShapeDtypeStruct((t, LANES), F32)
    spec = pl.BlockSpec((tm, LANES), lambda i: (i, 0))
    return pl.pallas_call(
        _table_kernel,
        grid=(t // tm,),
        in_specs=[pl.BlockSpec((tm, 1), lambda i: (i, 0)), pl.BlockSpec((8, LANES), lambda i: (0, 0))],
        out_specs=[spec] * 4,
        out_shape=[tab] * 4,
        name="rotary_tables",
    )(pos_col, inv)


def _ffn_kernel(*refs, with_mix, with_final):
    refs = list(refs)
    x_ref = refs.pop(0)
    if with_mix:
        yr_ref, ym_ref, yc_ref, wo_ref = refs[:4]
        refs = refs[4:]
    g_ref, wg_ref, wu_ref, wd_ref = refs[:4]
    refs = refs[4:]
    if with_final:
        fg_ref = refs.pop(0)
    (o_ref,) = refs

    for r in range(0, x_ref.shape[0], FFN_SUB_ROWS):
        rs = slice(r, r + FFN_SUB_ROWS)
        x = x_ref[rs, :]
        if with_mix:
            mixed = jnp.concatenate([yr_ref[rs, :], ym_ref[rs, :], yc_ref[rs, :]], axis=1)
            x = x + _dot(mixed, wo_ref[...])
        h = _rms(x, g_ref[...]).astype(BF16)
        a = (_silu(_dot(h, wg_ref[...])) * _dot(h, wu_ref[...])).astype(BF16)
        y = x + FFN_RES_WEIGHT * _dot(a, wd_ref[...])
        if with_final:
            y = _rms(y, fg_ref[...])
        o_ref[rs, :] = y


FFN_SUB_ROWS = 256
FFN_SUBTILES = 4


def _layer_resident(w, layer):
    return pl.BlockSpec((None,) + w.shape[1:], lambda i: (layer,) + (0,) * (w.ndim - 1),
                        pipeline_mode=pl.Buffered(1))


def _ffn(x, layer, g, wg, wu, wd, mix=None, final_g=None):
    t, d = x.shape
    tm = FFN_SUBTILES * FFN_SUB_ROWS
    with_mix = mix is not None
    with_final = final_g is not None
    row = lambda i: (i, 0)
    resident = lambda w: _layer_resident(w, layer)
    args = [x]
    specs = [pl.BlockSpec((tm, d), row)]
    if with_mix:
        yr, ym, yc, wo = mix
        args += [yr, ym, yc, wo]
        specs += [pl.BlockSpec((tm, yr.shape[1]), row), pl.BlockSpec((tm, ym.shape[1]), row),
                  pl.BlockSpec((tm, yc.shape[1]), row), resident(wo)]
    args += [g, wg, wu, wd]
    specs += [resident(g), resident(wg), resident(wu), resident(wd)]
    if with_final:
        args.append(final_g)
        specs.append(pl.BlockSpec(final_g.shape, lambda i: (0, 0)))
    return pl.pallas_call(
        functools.partial(_ffn_kernel, with_mix=with_mix, with_final=with_final),
        grid=(t // tm,),
        in_specs=specs,
        out_specs=pl.BlockSpec((tm, d), row),
        out_shape=jax.ShapeDtypeStruct((t, d), F32),
        compiler_params=pltpu.CompilerParams(dimension_semantics=("parallel",), vmem_limit_bytes=VMEM_LIMIT),
        name="ffn",
    )(*args)


def _swap_halves(z, half):
    n = z.shape[1]
    d = lax.broadcasted_iota(jnp.int32, z.shape, 1) % HEAD_DIM
    return jnp.where(d < half, pltpu.roll(z, n - half, 1), pltpu.roll(z, half, 1))


def _inproj_kernel(x_ref, g_ref, w_ref, cr_ref, sr_ref, cm_ref, sm_ref,
                   rq_ref, rk_ref, rv_ref, rg_ref, mq_ref, mk_ref, mvt_ref, ca_ref, cg_ref, km_ref,
                   *, rw, mw, cw):
    scale = HEAD_DIM ** -0.5

    def rot(z, c, s, half):
        reps = z.shape[1] // LANES
        c = jnp.concatenate([c] * reps, axis=1)
        s = jnp.concatenate([s] * reps, axis=1)
        return z * c + _swap_halves(z, half) * s

    for j in range(x_ref.shape[0] // MOBA_BLOCK):
        rs = slice(j * MOBA_BLOCK, (j + 1) * MOBA_BLOCK)
        z = _dot(_rms(x_ref[rs, :], g_ref[...]).astype(BF16), w_ref[...])
        cr, sr = cr_ref[rs, :], sr_ref[rs, :]
        cm, sm = cm_ref[rs, :], sm_ref[rs, :]
        off = 0
        rq_ref[rs, :] = rot(z[:, off:off + rw], cr, sr, HEAD_DIM // 2).astype(BF16)
        off += rw
        rk_ref[rs, :] = (rot(z[:, off:off + rw], cr, sr, HEAD_DIM // 2) * scale).astype(BF16)
        off += rw
        rv_ref[rs, :] = z[:, off:off + rw].astype(BF16)
        off += rw
        rg_ref[rs, :] = z[:, off:off + rw].astype(BF16)
        off += rw
        mq_ref[rs, :] = rot(z[:, off:off + mw], cm, sm, ROPE_DIMS // 2) * (scale * LOG2_E)
        off += mw
        mk = rot(z[:, off:off + mw], cm, sm, ROPE_DIMS // 2)
        mk_ref[rs, :] = mk.astype(BF16)
        km_ref[0, j:j + 1, :] = jnp.mean(mk, axis=0, keepdims=True)
        off += mw
        mvt_ref[:, rs] = z[:, off:off + mw].T.astype(BF16)
        off += mw
        ca_ref[rs, :] = z[:, off:off + cw]
        off += cw
        cg_ref[rs, :] = z[:, off:off + cw]


def _inproj(x, layer, g, w_in, tabs, rw, mw, cw, batch, seq):
    t, d = x.shape
    tm = 1024
    assert seq % tm == 0
    per_seq = seq // tm
    row = lambda i: (i, 0)
    nb = tm // MOBA_BLOCK
    out_shape = ([jax.ShapeDtypeStruct((t, rw), BF16)] * 4
                 + [jax.ShapeDtypeStruct((t, mw), F32), jax.ShapeDtypeStruct((t, mw), BF16),
                    jax.ShapeDtypeStruct((batch, mw, seq), BF16)]
                 + [jax.ShapeDtypeStruct((t, cw), F32)] * 2
                 + [jax.ShapeDtypeStruct((t // tm, nb, mw), F32)])
    out_specs = ([pl.BlockSpec((tm, rw), row)] * 4 + [pl.BlockSpec((tm, mw), row)] * 2
                 + [pl.BlockSpec((None, mw, tm), lambda i: (i // per_seq, 0, i % per_seq))]
                 + [pl.BlockSpec((tm, cw), row)] * 2 + [pl.BlockSpec((1, nb, mw), lambda i: (i, 0, 0))])
    tab_spec = pl.BlockSpec((tm, LANES), row)
    return pl.pallas_call(
        functools.partial(_inproj_kernel, rw=rw, mw=mw, cw=cw),
        grid=(t // tm,),
        in_specs=[pl.BlockSpec((tm, d), row), _layer_resident(g, layer), _layer_resident(w_in, layer)]
        + [tab_spec] * 4,
        out_specs=out_specs,
        out_shape=out_shape,
        compiler_params=pltpu.CompilerParams(dimension_semantics=("parallel",), vmem_limit_bytes=VMEM_LIMIT),
        name="inproj",
    )(x, g, w_in, *tabs)


CONV_SUB_ROWS = 128


def _retconv_kernel(q_ref, k_ref, v_ref, g_ref, a_ref, gt_ref, dec_ref, wc_ref, ws_ref, gam_ref, ones_ref,
                    cw_ref, cb_ref, lg_ref, lb_ref, cones_ref, o_ref, oc_ref, st_ref, u_ref, sh_ref):
    batch = q_ref.shape[0]
    pairs = q_ref.shape[2] // LANES
    ts = a_ref.shape[1]
    first_tap = CONV_HALO - (CONV_KERNEL - 1)

    @pl.when(pl.program_id(0) == 0)
    def _():
        st_ref[...] = jnp.zeros_like(st_ref)
        u_ref[:, 0:CONV_HALO, :] = jnp.zeros((batch, CONV_HALO, u_ref.shape[2]), F32)

    span = sh_ref.shape[2]
    for b in range(batch):
        u_ref[b, CONV_HALO:CONV_HALO + ts, :] = a_ref[b] * jax.nn.sigmoid(gt_ref[b])
        for res in range(1, SUBLANES):
            sh_ref[b, res - 1] = u_ref[b, res:res + span, :]

    def conv_rows(b, r):
        n = CONV_SUB_ROWS
        acc = jnp.broadcast_to(cb_ref[...], (n, u_ref.shape[2]))
        for kk in range(CONV_KERNEL):
            res, base = (first_tap + kk) % SUBLANES, (first_tap + kk) // SUBLANES * SUBLANES
            rows = (u_ref[b, r + base:r + base + n, :] if res == 0
                    else sh_ref[b, res - 1, r + base:r + base + n, :])
            acc = acc + cw_ref[kk:kk + 1, :] * rows
        mu = _group_mean(acc, cones_ref[...])
        dlt = acc - mu
        var = _group_mean(dlt * dlt, cones_ref[...])
        yn = dlt * lax.rsqrt(var + EPS) * lg_ref[...] + lb_ref[...]
        oc_ref[b, r:r + n, :] = _silu(yn).astype(BF16)

    def chain(b, p):
        ls = slice(p * LANES, (p + 1) * LANES)
        q, k, v = q_ref[b, :, ls], k_ref[b, :, ls], v_ref[b, :, ls]
        lane = lax.broadcasted_iota(jnp.int32, q.shape, 1)
        first = lane < HEAD_DIM
        zero = jnp.zeros_like(q)
        p0 = (_dot_nt(jnp.where(first, q, zero), k) * dec_ref[2 * p]).astype(BF16)
        p1 = (_dot_nt(jnp.where(first, zero, q), k) * dec_ref[2 * p + 1]).astype(BF16)
        inner = jnp.where(first, _dot(p0, v), _dot(p1, v))

        ch = b * pairs + p
        st = st_ref[ch]
        qc = (q.astype(F32) * wc_ref[p]).astype(BF16)
        y = inner + _dot(qc, st.astype(BF16))

        kw = (k.astype(F32) * ws_ref[p]).astype(BF16)
        r = lax.broadcasted_iota(jnp.int32, st.shape, 0) < HEAD_DIM
        cc = lax.broadcasted_iota(jnp.int32, st.shape, 1) < HEAD_DIM
        st_ref[ch] = st * gam_ref[p] + jnp.where(r == cc, _dot_tn(kw, v), 0.0)

        ms = _group_mean(y * y, ones_ref[...])
        yn = y * lax.rsqrt(ms + EPS)
        o_ref[b, :, ls] = (_silu(g_ref[b, :, ls].astype(F32)) * yn).astype(BF16)

    conv_jobs = [(b, r) for b in range(batch) for r in range(0, ts, CONV_SUB_ROWS)]
    chains = [(b, p) for b in range(batch) for p in range(pairs)]
    per_chain = -(-len(conv_jobs) // len(chains))
    for job in chains:
        chain(*job)
        for _ in range(min(per_chain, len(conv_jobs))):
            conv_rows(*conv_jobs.pop(0))
    for job in conv_jobs:
        conv_rows(*job)
    for b in range(batch):
        u_ref[b, 0:CONV_HALO, :] = u_ref[b, ts:ts + CONV_HALO, :]


def _retention_conv(rq, rk, rv, rg, ca, cg, ret_consts, conv_params, batch, seq):
    dec, wc, ws, gam, ones = ret_consts
    t, rw = rq.shape
    cw = ca.shape[1]
    c = RET_CHUNK
    pairs = rw // LANES
    blk = lambda width: pl.BlockSpec((batch, c, width), lambda i: (0, i, 0))
    whole = lambda a: pl.BlockSpec(a.shape, lambda i: (0,) * a.ndim)
    by_batch = lambda a: a.reshape(batch, seq, a.shape[1])
    y_ret, y_conv = pl.pallas_call(
        _retconv_kernel,
        grid=(seq // c,),
        in_specs=[blk(rw)] * 4 + [blk(cw)] * 2 + [whole(a) for a in (dec, wc, ws, gam, ones) + conv_params],
        out_specs=[blk(rw), blk(cw)],
        out_shape=[jax.ShapeDtypeStruct((batch, seq, rw), BF16), jax.ShapeDtypeStruct((batch, seq, cw), BF16)],
        scratch_shapes=[pltpu.VMEM((batch * pairs, LANES, LANES), F32),
                        pltpu.VMEM((batch, CONV_HALO + c, cw), F32),
                        pltpu.VMEM((batch, SUBLANES - 1, c + CONV_HALO - SUBLANES, cw), F32)],
        compiler_params=pltpu.CompilerParams(dimension_semantics=("arbitrary",)),
        name="retention_conv",
    )(*[by_batch(a) for a in (rq, rk, rv, rg, ca, cg)], dec, wc, ws, gam, ones, *conv_params)
    return y_ret.reshape(t, rw), y_conv.reshape(t, cw)


MOBA_ONES_ROWS = 16


def _moba_kernel(q_ref, k_ref, vt_ref, km_ref, o_ref, qa_ref, acc_ref, s0_ref, s1_ref, p0_ref, p1_ref, al_ref):
    i = pl.program_id(2)
    bs = MOBA_BLOCK
    nblk = k_ref.shape[0] // bs
    rows = 2 * bs
    hd = HEAD_DIM
    tiles = (nblk - 1 - i, i)

    s_refs = (s0_ref, s1_ref)
    p_refs = (p0_ref, p1_ref)
    neg_inf = jnp.float32(-jnp.inf)
    ones_rows = jnp.ones((MOBA_ONES_ROWS, bs), BF16)

    def pv_heads(st, p_t):
        return [_dot(jnp.concatenate([vt_ref[h * hd:(h + 1) * hd, pl.ds(st, bs)], ones_rows], axis=0),
                     p_t[:, h * bs:(h + 1) * bs]) for h in range(2)]

    def prep(tile):
        own = pl.multiple_of(tile * bs, bs)
        q_t = q_ref[pl.ds(own, bs), :].T
        q_t = jnp.concatenate([q_t, q_t], axis=1)
        dim_first = lax.broadcasted_iota(jnp.int32, q_t.shape, 0) < hd
        col_first = lax.broadcasted_iota(jnp.int32, q_t.shape, 1) < bs
        qs_t = jnp.where(dim_first == col_first, q_t, 0.0)

        qh, ql = _split_bf16(qs_t)
        kh, kl = _split_bf16(km_ref[0])
        gate = _dot(kh, qh) + _dot(kh, ql) + _dot(kl, qh)
        nidx = lax.broadcasted_iota(jnp.int32, gate.shape, 0).astype(F32)
        gate = jnp.where(nidx < tile.astype(F32), gate, neg_inf)
        bias = jnp.full(gate.shape, NEG_BIG, F32)
        for _ in range(MOBA_TOPK):
            mx = jnp.max(gate, axis=0, keepdims=True)
            pick = jnp.min(jnp.where(gate == mx, nidx, float(LANES)), axis=0, keepdims=True)
            hit = nidx == pick
            bias = jnp.where(hit & (mx > neg_inf), 0.0, bias)
            gate = jnp.where(hit, neg_inf, gate)
        return jnp.concatenate([qs_t.astype(BF16), bias.astype(BF16),
                                jnp.full((LANES - nblk, rows), NEG_BIG, BF16)], axis=0)

    n_long = tiles[0]
    half = nblk // 2
    seq = [("own", 0)] + [("past", s) for s in range(half)] + [("own", 1)] + [("past", s) for s in range(half, nblk - 1)]
    sel_row = lax.broadcasted_iota(jnp.int32, (16, LANES), 1)

    def step_of(step):
        kind, v = step
        if kind == "own":
            return True, None, v, None, pl.multiple_of(tiles[v] * bs, bs)
        if v < half:
            return False, False, 0, v, v * bs
        is_b = v >= n_long
        blk = jnp.where(is_b, v - n_long, v)
        return False, is_b, is_b.astype(jnp.int32), blk, pl.multiple_of(blk * bs, bs)

    def stage_a(step, slot):
        own, _, w, blk, st = step_of(step)
        if own:
            sel = jnp.zeros((bs, LANES), BF16)
        else:
            sel = jnp.tile(jnp.where(sel_row == blk, 1.0, 0.0).astype(BF16), (bs // 16, 1))
        sc = _dot(jnp.concatenate([k_ref[pl.ds(st, bs), :], sel], axis=1), qa_ref[w])
        if own:
            kpos = lax.broadcasted_iota(jnp.int32, sc.shape, 0)
            qpos = lax.broadcasted_iota(jnp.int32, sc.shape, 1) % bs
            sc = jnp.where(kpos <= qpos, sc, NEG_BIG)
        s_refs[slot][...] = sc

    def stage_b(step, slot, m_pair):
        own, is_b, w, _, _ = step_of(step)
        sc = s_refs[slot][...]
        m_loc = jnp.max(sc, axis=0, keepdims=True)
        if own:
            m_new = m_loc
            m_pair = tuple(m_new if j == w else m_pair[j] for j in range(2))
        else:
            if is_b is False:
                m_old = m_pair[0]
            else:
                m_old = jnp.where(is_b, m_pair[1], m_pair[0])
            m_new = jnp.maximum(m_old, m_loc)
            al_ref[slot:slot + 1, :] = jnp.exp2(m_old - m_new)
            if is_b is False:
                m_pair = (m_new, m_pair[1])
            else:
                m_pair = (jnp.where(is_b, m_pair[0], m_new), jnp.where(is_b, m_new, m_pair[1]))
        p_refs[slot][...] = jnp.exp2(sc - m_new).astype(BF16)
        return m_pair

    def stage_c(step, slot):
        own, _, w, _, st = step_of(step)
        pv = pv_heads(st, p_refs[slot][...])
        if own:
            for h in range(2):
                acc_ref[w, h] = pv[h]
        else:
            al = al_ref[slot:slot + 1, :]
            for h in range(2):
                acc_ref[w, h] = acc_ref[w, h] * al[:, h * bs:(h + 1) * bs] + pv[h]

    for w in range(2):
        qa_ref[w] = prep(tiles[w])
    m_pair = (None, None)
    steps = len(seq)
    for t in range(steps + 2):
        if t < steps:
            stage_a(seq[t], t % 2)
        if 1 <= t <= steps:
            m_pair = stage_b(seq[t - 1], (t - 1) % 2, m_pair)
        if t >= 2:
            stage_c(seq[t - 2], t % 2)

    for w in range(2):
        outs = []
        for h in range(2):
            acc = acc_ref[w, h]
            outs.append(acc[0:hd, :] * (1.0 / acc[hd:hd + 1, :]))
        own = pl.multiple_of(tiles[w] * bs, bs)
        o_ref[pl.ds(own, bs), :] = jnp.concatenate(outs, axis=0).T.astype(BF16)


def _moba(mq, mk, mvt, kmean, batch, seq):
    t, mw = mq.shape
    bs = MOBA_BLOCK
    nq = seq // bs
    assert nq % 2 == 0
    pairs = mw // LANES
    seqblk = pl.BlockSpec((seq, LANES), lambda b, p, i: (b, p))
    return pl.pallas_call(
        _moba_kernel,
        grid=(batch, pairs, nq // 2),
        in_specs=[seqblk, seqblk, pl.BlockSpec((None, LANES, seq), lambda b, p, i: (b, p, 0)),
                  pl.BlockSpec((1, nq, LANES), lambda b, p, i: (b, 0, p))],
        out_specs=seqblk,
        out_shape=jax.ShapeDtypeStruct((t, mw), BF16),
        scratch_shapes=[pltpu.VMEM((2, 2 * LANES, 2 * bs), BF16),
                        pltpu.VMEM((2, 2, HEAD_DIM + MOBA_ONES_ROWS, bs), F32),
                        pltpu.VMEM((bs, 2 * bs), F32), pltpu.VMEM((bs, 2 * bs), F32),
                        pltpu.VMEM((bs, 2 * bs), BF16), pltpu.VMEM((bs, 2 * bs), BF16),
                        pltpu.VMEM((8, 2 * bs), F32)],
        compiler_params=pltpu.CompilerParams(dimension_semantics=("parallel", "parallel", "arbitrary"),
                                             vmem_limit_bytes=VMEM_LIMIT),
        name="moba",
    )(mq, mk, mvt, kmean)


def _lane_constants():
    d = np.arange(LANES) % HEAD_DIM
    half_ret = HEAD_DIM // 2
    ret_inv = RET_ROT_BASE ** (-jnp.linspace(0.0, 1.0, half_ret, dtype=F32))
    half_rope = ROPE_DIMS // 2
    rope_inv = ROPE_THETA ** (-jnp.arange(half_rope, dtype=F32) / half_rope)
    inv_r = ret_inv[d % half_ret]
    sgn_r = jnp.asarray(np.where(d < half_ret, -1.0, 1.0), F32)
    inv_m = jnp.where(jnp.asarray(d < ROPE_DIMS), rope_inv[d % half_rope], 0.0)
    sgn_m = jnp.asarray(np.where(d < half_rope, -1.0, 1.0), F32)
    zero = jnp.zeros((LANES,), F32)
    return jnp.stack([inv_r, sgn_r, inv_m, sgn_m, zero, zero, zero, zero])


def _retention_constants(heads):
    c = RET_CHUNK
    pairs = heads // 2
    log_gamma = jnp.log1p(-jnp.exp2(-5.0 - jnp.arange(heads, dtype=F32)))
    lg = log_gamma[:, None]
    idx = jnp.arange(c, dtype=F32)
    rel = idx[:, None] - idx[None, :]
    decay_in = jnp.where(rel >= 0, jnp.exp(lg[:, :, None] * jnp.maximum(rel, 0.0)), 0.0)
    w_state = jnp.exp(lg * (c - 1 - idx))
    w_cross = jnp.exp(lg * (idx + 1.0))
    chunk_decay = jnp.exp(log_gamma * c)

    def lanes(w):
        return jnp.repeat(w.reshape(pairs, 2, c).transpose(0, 2, 1), HEAD_DIM, axis=2)

    gam = jnp.repeat(chunk_decay.reshape(pairs, 1, 2), HEAD_DIM, axis=2)
    gam = jnp.broadcast_to(gam, (pairs, LANES, LANES))
    return decay_in, lanes(w_cross), lanes(w_state), gam


def _group_ones(width):
    g = np.arange(width) // HEAD_DIM
    return jnp.asarray((g[:, None] == g[None, :]) / HEAD_DIM, BF16)


def kernel(x, positions, ffn1_norm, ffn1_wg, ffn1_wu, ffn1_wd, mix_norm, w_in, conv_w, conv_b, conv_ln_g,
           conv_ln_b, w_out, ffn2_norm, ffn2_wg, ffn2_wu, ffn2_wd, final_norm):
    batch, seq, d = x.shape
    depth = w_in.shape[0]
    cw = conv_w.shape[2]
    rw = mw = (w_out.shape[1] - cw) // 2
    assert seq % MOBA_BLOCK == 0 and seq % RET_CHUNK == 0
    assert w_in.shape[2] == 4 * rw + 3 * mw + 2 * cw
    t = batch * seq

    tabs = _rotary_tables(positions.reshape(t, 1), _lane_constants())
    ret_consts = _retention_constants(rw // HEAD_DIM) + (_group_ones(LANES),)
    conv_ones = _group_ones(cw)

    xf = x.reshape(t, d)
    row = lambda v: v.reshape(1, -1)
    gains = lambda v: v.reshape(depth, 1, d)
    ffn1 = (gains(ffn1_norm), ffn1_wg.astype(BF16), ffn1_wu.astype(BF16), ffn1_wd.astype(BF16))
    ffn2 = (gains(ffn2_norm), ffn2_wg.astype(BF16), ffn2_wu.astype(BF16), ffn2_wd.astype(BF16))
    w_in_b, w_out_b, mix_g = w_in.astype(BF16), w_out.astype(BF16), gains(mix_norm)
    for l in range(depth):
        xf = _ffn(xf, l, *ffn1)
        rq, rk, rv, rg, mq, mk, mvt, ca, cg, km = _inproj(xf, l, mix_g, w_in_b, tabs, rw, mw, cw, batch, seq)
        y_ret, y_conv = _retention_conv(
            rq, rk, rv, rg, ca, cg, ret_consts,
            (conv_w[l], row(conv_b[l]), row(conv_ln_g[l]), row(conv_ln_b[l]), conv_ones), batch, seq)
        y_moba = _moba(mq, mk, mvt, km.reshape(batch, seq // MOBA_BLOCK, mw), batch, seq)
        xf = _ffn(xf, l, *ffn2, mix=(y_ret, y_moba, y_conv, w_out_b),
                  final_g=row(final_norm) if l == depth - 1 else None)
    return xf.reshape(batch, seq, d)
```
